```python
import math
import jax, jax.numpy as jnp
from jax import lax
import numpy as np

D_MODEL = 1024
BATCH = 4
SEQ = 4096
DEPTH = 1

EPS = 1e-6
CONV_CH = D_MODEL
CONV_WIDTH = 31
N_HEADS = 8
QK_NOPE = 128
QK_ROPE = 64
V_DIM = 128
Q_LORA = 256
KV_LORA = 128
ROPE_BASE = 10000.0
Q_BLOCK = 128
N_KEYS = 128
N_EXPERTS = N_KEYS * N_KEYS
PEER_HEADS = 8
PEER_DK = 256
PEER_TOPK = 16
TOKEN_BLOCK = 128
IN_SPLITS = (CONV_CH, 2 * CONV_CH, 2 * CONV_CH + Q_LORA, 2 * CONV_CH + Q_LORA + KV_LORA + QK_ROPE)
IN_WIDTH = 2 * CONV_CH + Q_LORA + KV_LORA + QK_ROPE + 2 * D_MODEL

kernel_name = "hybrid_conv_mla_peer_block"


def rmsnorm(x, g):
    xf = x.astype(jnp.float32)
    y = xf * lax.rsqrt(jnp.mean(xf * xf, axis=-1, keepdims=True) + EPS)
    return (y * g.astype(jnp.float32)).astype(x.dtype)


def layernorm(x, g, b):
    xf = x.astype(jnp.float32)
    mu = jnp.mean(xf, axis=-1, keepdims=True)
    var = jnp.mean(jnp.square(xf - mu), axis=-1, keepdims=True)
    y = (xf - mu) * lax.rsqrt(var + EPS)
    return (y * g.astype(jnp.float32) + b.astype(jnp.float32)).astype(x.dtype)


def apply_rope(t, positions):
    half = t.shape[-1] // 2
    inv_freq = ROPE_BASE ** (-jnp.arange(half, dtype=jnp.float32) / half)
    ang = positions.astype(jnp.float32)[..., None] * inv_freq
    cos = jnp.cos(ang)[:, :, None, :]
    sin = jnp.sin(ang)[:, :, None, :]
    tf = t.astype(jnp.float32)
    t1, t2 = tf[..., :half], tf[..., half:]
    out = jnp.concatenate([t1 * cos - t2 * sin, t2 * cos + t1 * sin], axis=-1)
    return out.astype(t.dtype)


def conformer_conv(val, gate, w_dw, b_dw, ln_g, ln_b, w_pw):
    h = val * jax.nn.sigmoid(gate)
    h = lax.conv_general_dilated(
        h, w_dw, window_strides=(1,), padding=[(CONV_WIDTH - 1, 0)],
        dimension_numbers=("NWC", "WIO", "NWC"), feature_group_count=CONV_CH) + b_dw
    h = jax.nn.silu(layernorm(h, ln_g, ln_b))
    return h @ w_pw


def mla(c_q, c_kv_full, positions, q_norm_g, w_uq, kv_norm_g, w_ukv):
    B, S, _ = c_q.shape
    c_kv, k_rope = c_kv_full[..., :KV_LORA], c_kv_full[..., KV_LORA:]
    q = (rmsnorm(c_q, q_norm_g) @ w_uq).reshape(B, S, N_HEADS, QK_NOPE + QK_ROPE)
    q_nope, q_rope = q[..., :QK_NOPE], apply_rope(q[..., QK_NOPE:], positions)
    kv = (rmsnorm(c_kv, kv_norm_g) @ w_ukv).reshape(B, S, N_HEADS, QK_NOPE + V_DIM)
    k_nope, v = kv[..., :QK_NOPE], kv[..., QK_NOPE:]
    k_rope = apply_rope(k_rope[:, :, None, :], positions)[:, :, 0, :]
    scale = 1.0 / math.sqrt(QK_NOPE + QK_ROPE)
    nb = S // Q_BLOCK
    qn_b = jnp.moveaxis(q_nope.reshape(B, nb, Q_BLOCK, N_HEADS, QK_NOPE), 1, 0)
    qr_b = jnp.moveaxis(q_rope.reshape(B, nb, Q_BLOCK, N_HEADS, QK_ROPE), 1, 0)
    starts = jnp.arange(nb, dtype=jnp.int32) * Q_BLOCK
    k_pos = jnp.arange(S, dtype=jnp.int32)

    def attend(args):
        qn, qr, start = args
        s = (jnp.einsum("bqhd,bkhd->bhqk", qn, k_nope)
             + jnp.einsum("bqhr,bkr->bhqk", qr, k_rope)).astype(jnp.float32) * scale
        q_pos = start + jnp.arange(Q_BLOCK, dtype=jnp.int32)
        mask = k_pos[None, :] <= q_pos[:, None]
        s = jnp.where(mask[None, None], s, -1e30)
        p = jax.nn.softmax(s, axis=-1).astype(v.dtype)
        return jnp.einsum("bhqk,bkhd->bqhd", p, v)

    o = lax.map(attend, (qn_b, qr_b, starts))
    return jnp.moveaxis(o, 0, 1).reshape(B, S, N_HEADS * V_DIM)


def peer(x, w_pq, sub_keys, u, v):
    B, S, D = x.shape
    xb_all = x.reshape(-1, TOKEN_BLOCK, D)

    def block(xb):
        q = (xb @ w_pq).reshape(TOKEN_BLOCK, PEER_HEADS, 2, PEER_DK // 2)
        s = jnp.einsum("thpd,hpkd->thpk", q, sub_keys).astype(jnp.float32)
        s_top, i_top = lax.top_k(s, PEER_TOPK)
        cand_s = s_top[:, :, 0, :, None] + s_top[:, :, 1, None, :]
        cand_i = i_top[:, :, 0, :, None] * N_KEYS + i_top[:, :, 1, None, :]
        cand_s = cand_s.reshape(TOKEN_BLOCK, PEER_HEADS, PEER_TOPK * PEER_TOPK)
        cand_i = cand_i.reshape(TOKEN_BLOCK, PEER_HEADS, PEER_TOPK * PEER_TOPK)
        best_s, best_pos = lax.top_k(cand_s, PEER_TOPK)
        idx = jnp.take_along_axis(cand_i, best_pos, axis=-1)
        g = jax.nn.softmax(best_s, axis=-1)
        ue = jnp.take(u, idx, axis=0)
        act = jax.nn.gelu(jnp.einsum("thkd,td->thk", ue, xb).astype(jnp.float32), approximate=False)
        ve = jnp.take(v, idx, axis=0)
        return jnp.einsum("thk,thkd->td", (g * act).astype(ve.dtype), ve)

    return lax.map(block, xb_all).reshape(B, S, D)


def setup_inputs(seed: int = 0) -> dict:
    key = jax.random.key(seed)
    ks = jax.random.split(key, 24)
    f32 = jnp.float32
    nrm = lambda k, shape, scale: jax.random.normal(k, shape, f32) * scale
    L = DEPTH
    x = jax.random.normal(ks[0], (BATCH, SEQ, D_MODEL), f32)
    offset = jax.random.randint(ks[1], (BATCH, 1), 0, 1024, dtype=jnp.int32)
    positions = (offset + jnp.arange(SEQ, dtype=jnp.int32)[None, :]).astype(jnp.int32)
    return {
        "x": x,
        "positions": positions,
        "mix_norm_g": 1.0 + nrm(ks[2], (L, D_MODEL), 0.02),
        "w_in": nrm(ks[3], (L, D_MODEL, IN_WIDTH), D_MODEL ** -0.5),
        "b_gate": nrm(ks[4], (L, 2 * D_MODEL), 0.1),
        "w_dw": nrm(ks[5], (L, CONV_WIDTH, 1, CONV_CH), CONV_WIDTH ** -0.5),
        "b_dw": nrm(ks[6], (L, CONV_CH), 0.02),
        "conv_ln_g": 1.0 + nrm(ks[7], (L, CONV_CH), 0.02),
        "conv_ln_b": nrm(ks[8], (L, CONV_CH), 0.02),
        "w_conv_out": nrm(ks[9], (L, CONV_CH, D_MODEL), CONV_CH ** -0.5),
        "q_norm_g": 1.0 + nrm(ks[10], (L, Q_LORA), 0.02),
        "w_uq": nrm(ks[11], (L, Q_LORA, N_HEADS * (QK_NOPE + QK_ROPE)), Q_LORA ** -0.5),
        "kv_norm_g": 1.0 + nrm(ks[12], (L, KV_LORA), 0.02),
        "w_ukv": nrm(ks[13], (L, KV_LORA, N_HEADS * (QK_NOPE + V_DIM)), KV_LORA ** -0.5),
        "w_attn_out": nrm(ks[14], (L, N_HEADS * V_DIM, D_MODEL), (N_HEADS * V_DIM) ** -0.5),
        "w_o": nrm(ks[15], (L, D_MODEL, D_MODEL), D_MODEL ** -0.5),
        "ffn_norm_g": 1.0 + nrm(ks[16], (L, D_MODEL), 0.02),
        "w_peer_q": nrm(ks[17], (L, D_MODEL, PEER_HEADS * PEER_DK), D_MODEL ** -0.5),
        "peer_sub_keys": nrm(ks[18], (L, PEER_HEADS, 2, N_KEYS, PEER_DK // 2), (PEER_DK // 2) ** -0.5),
        "peer_u": nrm(ks[19], (L, N_EXPERTS, D_MODEL), D_MODEL ** -0.5),
        "peer_v": nrm(ks[20], (L, N_EXPERTS, D_MODEL), 0.5),
        "final_norm_g": 1.0 + nrm(ks[21], (D_MODEL,), 0.02),
    }


def reference(x, positions, mix_norm_g, w_in, b_gate, w_dw, b_dw, conv_ln_g, conv_ln_b, w_conv_out,
              q_norm_g, w_uq, kv_norm_g, w_ukv, w_attn_out, w_o, ffn_norm_g, w_peer_q,
              peer_sub_keys, peer_u, peer_v, final_norm_g):
    h = x
    for l in range(DEPTH):
        a = rmsnorm(h, mix_norm_g[l])
        z = a @ w_in[l]
        conv_val, conv_gate, c_q, c_kv, gate_logits = jnp.split(z, IN_SPLITS, axis=-1)
        y_conv = conformer_conv(conv_val, conv_gate, w_dw[l], b_dw[l], conv_ln_g[l], conv_ln_b[l], w_conv_out[l])
        y_attn = mla(c_q, c_kv, positions, q_norm_g[l], w_uq[l], kv_norm_g[l], w_ukv[l]) @ w_attn_out[l]
        gates = jax.nn.sigmoid(gate_logits + b_gate[l])
        g_conv, g_attn = gates[..., :D_MODEL], gates[..., D_MODEL:]
        h = h + (g_conv * y_conv + g_attn * y_attn) @ w_o[l]
        h = h + peer(rmsnorm(h, ffn_norm_g[l]), w_peer_q[l], peer_sub_keys[l], peer_u[l], peer_v[l])
    return rmsnorm(h, final_norm_g)
```

```python
import functools
import math

import jax
import jax.numpy as jnp
from jax import lax
from jax.experimental import pallas as pl
from jax.experimental.pallas import tpu as pltpu

F32 = jnp.float32
BF16 = jnp.bfloat16

EPS = 1e-6
CONV_WIDTH = 31
N_HEADS = 8
QK_NOPE = 128
QK_ROPE = 64
V_DIM = 128
Q_LORA = 256
KV_LORA = 128
ROPE_BASE = 10000.0
N_KEYS = 128
PEER_HEADS = 8
PEER_TOPK = 16
QK_PAD = 256
LANE = 128
HALO = 32
VMEM_LIMIT = 56 * 1024 * 1024


def _cparams(sem):
    return pltpu.CompilerParams(dimension_semantics=sem, vmem_limit_bytes=VMEM_LIMIT)


def _dot(a, b):
    return jnp.dot(a, b, preferred_element_type=F32)


def _dot_nt(a, b):
    return lax.dot_general(a, b, (((1,), (1,)), ((), ())), preferred_element_type=F32)


def _sigmoid(x):
    return 1.0 / (1.0 + jnp.exp(-x))


def _rms(x, g):
    return x * lax.rsqrt(jnp.mean(x * x, axis=-1, keepdims=True) + EPS) * g


def _inproj_kernel(x_ref, g_ref, wv_ref, wg_ref, wq_ref, wkv_ref, wgl_ref, bg_ref,
                   hglu_ref, cq_ref, ckv_ref, gates_ref):
    a = _rms(x_ref[...], g_ref[...]).astype(BF16)
    val = _dot(a, wv_ref[...])
    gate = _dot(a, wg_ref[...])
    hglu_ref[...] = val * _sigmoid(gate)
    cq_ref[...] = _dot(a, wq_ref[...])
    ckv_ref[...] = _dot(a, wkv_ref[...])
    gates_ref[...] = _sigmoid(_dot(a, wgl_ref[...]) + bg_ref[...]).astype(BF16)


def _in_proj(x2, g, wv, wg, wq, wkv, wgl, bg, *, tm):
    T, D = x2.shape
    full = lambda a: pl.BlockSpec(a.shape, lambda i: (0, 0))
    row = lambda n: pl.BlockSpec((tm, n), lambda i: (i, 0))
    return pl.pallas_call(
        _inproj_kernel,
        grid=(T // tm,),
        in_specs=[row(D), full(g), full(wv), full(wg), full(wq), full(wkv), full(wgl), full(bg)],
        out_specs=[row(wv.shape[1]), row(wq.shape[1]), row(wkv.shape[1]), row(wgl.shape[1])],
        out_shape=[jax.ShapeDtypeStruct((T, wv.shape[1]), F32),
                   jax.ShapeDtypeStruct((T, wq.shape[1]), F32),
                   jax.ShapeDtypeStruct((T, wkv.shape[1]), F32),
                   jax.ShapeDtypeStruct((T, wgl.shape[1]), BF16)],
        compiler_params=_cparams(("parallel",)),
        name="in_proj",
    )(x2, g, wv, wg, wq, wkv, wgl, bg)


def _mla_proj_kernel(cq_ref, ckv_ref, pos_ref, invf_ref, qg_ref, wqa_ref, wqr_ref, kvg_ref, wk_ref, wv_ref,
                     q_ref, k_ref, v_ref, *, scale):
    qn = _rms(cq_ref[...], qg_ref[...]).astype(BF16)
    ckv_full = ckv_ref[...]
    kvn = _rms(ckv_full[:, :KV_LORA], kvg_ref[...]).astype(BF16)
    ang = pos_ref[...].astype(F32) * invf_ref[...]
    cosv = jnp.cos(ang)
    sinv = jnp.sin(ang)
    kr = (ckv_full[:, KV_LORA:KV_LORA + LANE] * cosv + ckv_full[:, KV_LORA + LANE:] * sinv).astype(BF16)
    qa = _dot(qn, wqa_ref[...])
    qr = _dot(qn, wqr_ref[...])
    kn = _dot(kvn, wk_ref[...])
    vv = _dot(kvn, wv_ref[...])
    for h in range(N_HEADS):
        q_ref[0, h, :, 0:LANE] = (qa[:, h * QK_PAD:h * QK_PAD + LANE] * scale).astype(BF16)
        q_rope = qa[:, h * QK_PAD + LANE:(h + 1) * QK_PAD] * cosv + qr[:, h * LANE:(h + 1) * LANE] * sinv
        q_ref[0, h, :, LANE:QK_PAD] = (q_rope * scale).astype(BF16)
        k_ref[0, h, :, 0:LANE] = kn[:, h * LANE:(h + 1) * LANE].astype(BF16)
        k_ref[0, h, :, LANE:QK_PAD] = kr
        v_ref[0, h, :, :] = vv[:, h * V_DIM:(h + 1) * V_DIM].astype(BF16)


def _mla_proj(cq, ckv, pos2, invf, qg, wqa, wqr, kvg, wk, wv, *, B, S, tm):
    T = cq.shape[0]
    nb = S // tm
    full = lambda a: pl.BlockSpec(a.shape, lambda i: (0, 0))
    row = lambda n: pl.BlockSpec((tm, n), lambda i: (i, 0))
    head = lambda n: pl.BlockSpec((1, N_HEADS, tm, n), lambda i: (i // nb, 0, i % nb, 0))
    scale = 1.0 / math.sqrt(QK_NOPE + QK_ROPE)
    return pl.pallas_call(
        functools.partial(_mla_proj_kernel, scale=scale),
        grid=(T // tm,),
        in_specs=[row(cq.shape[1]), row(ckv.shape[1]), row(1), full(invf), full(qg), full(wqa), full(wqr),
                  full(kvg), full(wk), full(wv)],
        out_specs=[head(QK_PAD), head(QK_PAD), head(V_DIM)],
        out_shape=[jax.ShapeDtypeStruct((B, N_HEADS, S, QK_PAD), BF16),
                   jax.ShapeDtypeStruct((B, N_HEADS, S, QK_PAD), BF16),
                   jax.ShapeDtypeStruct((B, N_HEADS, S, V_DIM), BF16)],
        compiler_params=_cparams(("parallel",)),
        name="mla_proj",
    )(cq, ckv, pos2, invf, qg, wqa, wqr, kvg, wk, wv)


def _flash_kernel(q_ref, k_ref, v_ref, o_ref, m_scr, l_scr, acc_scr, *, bq, bk):
    qi = pl.program_id(2)
    ki = pl.program_id(3)

    @pl.when(ki == 0)
    def _():
        m_scr[...] = jnp.full(m_scr.shape, -1e30, F32)
        l_scr[...] = jnp.zeros(l_scr.shape, F32)
        acc_scr[...] = jnp.zeros(acc_scr.shape, F32)

    @pl.when(ki <= qi)
    def _():
        s = _dot_nt(q_ref[0, 0], k_ref[0, 0])
        qpos = qi * bq + lax.broadcasted_iota(jnp.int32, (bq, bk), 0)
        kpos = ki * bk + lax.broadcasted_iota(jnp.int32, (bq, bk), 1)
        s = jnp.where(kpos <= qpos, s, -1e30)
        m_prev = m_scr[...]
        m_new = jnp.maximum(m_prev, jnp.max(s, axis=1, keepdims=True))
        p = jnp.exp(s - m_new)
        alpha = jnp.exp(m_prev - m_new)
        l_scr[...] = alpha * l_scr[...] + jnp.sum(p, axis=1, keepdims=True)
        acc_scr[...] = alpha * acc_scr[...] + _dot(p.astype(BF16), v_ref[0, 0])
        m_scr[...] = m_new

    @pl.when(ki == qi)
    def _():
        o_ref[0] = (acc_scr[...] / l_scr[...]).astype(BF16)


def _flash_attn(q, k, v, *, bq, bk):
    B, H, S, _ = q.shape
    assert bq == bk
    return pl.pallas_call(
        functools.partial(_flash_kernel, bq=bq, bk=bk),
        grid=(B, H, S // bq, S // bk),
        in_specs=[pl.BlockSpec((1, 1, bq, QK_PAD), lambda b, h, i, j: (b, h, i, 0)),
                  pl.BlockSpec((1, 1, bk, QK_PAD), lambda b, h, i, j: (b, h, jnp.minimum(i, j), 0)),
                  pl.BlockSpec((1, 1, bk, V_DIM), lambda b, h, i, j: (b, h, jnp.minimum(i, j), 0))],
        out_specs=pl.BlockSpec((1, bq, V_DIM), lambda b, h, i, j: (b, i, h)),
        out_shape=jax.ShapeDtypeStruct((B, S, H * V_DIM), BF16),
        scratch_shapes=[pltpu.VMEM((bq, 1), F32), pltpu.VMEM((bq, 1), F32), pltpu.VMEM((bq, V_DIM), F32)],
        compiler_params=_cparams(("parallel", "parallel", "parallel", "arbitrary")),
        name="flash_attn",
    )(q, k, v)


def _mix_kernel(hc_ref, hp_ref, o_ref, gates_ref, x_ref, wdw_ref, bdw_ref, lng_ref, lnb_ref,
                wco_ref, wao_ref, wo_ref, h1_ref, hbuf, cbuf, *, tm, blocks_per_seq, rows):
    i = pl.program_id(0)
    first = (i % blocks_per_seq) == 0
    hbuf[0:HALO, :] = jnp.where(first, 0.0, hp_ref[...])
    hbuf[HALO:HALO + tm, :] = hc_ref[...]
    D = hc_ref.shape[1]
    base = HALO - (CONV_WIDTH - 1)
    for c in range(D // LANE):
        cs = slice(c * LANE, (c + 1) * LANE)
        for r in range(tm // rows):
            acc = jnp.broadcast_to(bdw_ref[:, cs], (rows, LANE))
            for kk in range(CONV_WIDTH):
                start = r * rows + base + kk
                acc = acc + wdw_ref[kk:kk + 1, cs] * hbuf[start:start + rows, cs]
            cbuf[r * rows:(r + 1) * rows, cs] = acc
    y = cbuf[...]
    mu = jnp.mean(y, axis=-1, keepdims=True)
    yc = y - mu
    var = jnp.mean(yc * yc, axis=-1, keepdims=True)
    yn = yc * lax.rsqrt(var + EPS) * lng_ref[...] + lnb_ref[...]
    act = (yn * _sigmoid(yn)).astype(BF16)
    y_conv = _dot(act, wco_ref[...])
    y_attn = _dot(o_ref[...], wao_ref[...])
    gts = gates_ref[...].astype(F32)
    mixed = (gts[:, :D] * y_conv + gts[:, D:] * y_attn).astype(BF16)
    h1_ref[...] = x_ref[...] + _dot(mixed, wo_ref[...])


def _mix(hglu, o2, gates, x2, wdw, bdw, lng, lnb, wco, wao, wo, *, S, tm):
    T, D = x2.shape
    full = lambda a: pl.BlockSpec(a.shape, lambda i: (0, 0))
    row = lambda n: pl.BlockSpec((tm, n), lambda i: (i, 0))
    per = tm // HALO
    halo = pl.BlockSpec((HALO, D), lambda i: (jnp.maximum(i * per - 1, 0), 0))
    return pl.pallas_call(
        functools.partial(_mix_kernel, tm=tm, blocks_per_seq=S // tm, rows=64),
        grid=(T // tm,),
        in_specs=[row(D), halo, row(D), row(2 * D), row(D), full(wdw), full(bdw), full(lng), full(lnb),
                  full(wco), full(wao), full(wo)],
        out_specs=row(D),
        out_shape=jax.ShapeDtypeStruct((T, D), F32),
        scratch_shapes=[pltpu.VMEM((HALO + tm, D), F32), pltpu.VMEM((tm, D), F32)],
        compiler_params=_cparams(("parallel",)),
        name="mix",
    )(hglu, hglu, o2, gates, x2, wdw, bdw, lng, lnb, wco, wao, wo)


def _peer_prep_kernel(h1_ref, g_ref, wpqt_ref, sk_ref, xn_ref, st_ref):
    xn = _rms(h1_ref[...], g_ref[...]).astype(BF16)
    xn_ref[...] = xn
    qt = _dot_nt(wpqt_ref[...], xn).astype(BF16)
    for hp in range(2 * PEER_HEADS):
        rs = slice(hp * N_KEYS, (hp + 1) * N_KEYS)
        st_ref[rs, :] = _dot(sk_ref[hp], qt[rs, :])


def _peer_prep(h1, g, wpqt, sk, *, tm):
    T, D = h1.shape
    R = wpqt.shape[0]
    return pl.pallas_call(
        _peer_prep_kernel,
        grid=(T // tm,),
        in_specs=[pl.BlockSpec((tm, D), lambda i: (i, 0)), pl.BlockSpec(g.shape, lambda i: (0, 0)),
                  pl.BlockSpec(wpqt.shape, lambda i: (0, 0)), pl.BlockSpec(sk.shape, lambda i: (0, 0, 0))],
        out_specs=[pl.BlockSpec((tm, D), lambda i: (i, 0)), pl.BlockSpec((R, tm), lambda i: (0, i))],
        out_shape=[jax.ShapeDtypeStruct((T, D), BF16), jax.ShapeDtypeStruct((R, T), F32)],
        compiler_params=_cparams(("parallel",)),
        name="peer_prep",
    )(h1, g, wpqt, sk)


def _top16(s, iota_k):
    work = s
    rank = jnp.full(s.shape, float(PEER_TOPK), F32)
    vals = []
    for a in range(PEER_TOPK):
        m = jnp.max(work, axis=0, keepdims=True)
        idx = jnp.min(jnp.where(work == m, iota_k, float(N_KEYS)), axis=0, keepdims=True)
        sel = iota_k == idx
        rank = jnp.where(sel, float(a), rank)
        work = jnp.where(sel, -jnp.inf, work)
        vals.append(m)
    return vals, rank


def _stack_rows(rows, iota_a):
    out = jnp.zeros(iota_a.shape, F32)
    for a, r in enumerate(rows):
        out = jnp.where(iota_a == float(a), r, out)
    return out


def _peer_topk_kernel(st_ref, r2_ref, lc_ref, e1_ref, e2_ref):
    tb = st_ref.shape[1]
    s1 = st_ref[0:N_KEYS, :]
    s2 = st_ref[N_KEYS:2 * N_KEYS, :]
    iota_k = lax.broadcasted_iota(jnp.int32, (N_KEYS, tb), 0).astype(F32)
    iota_a = lax.broadcasted_iota(jnp.int32, (PEER_TOPK, tb), 0).astype(F32)
    vals1, rank1 = _top16(s1, iota_k)
    vals2, rank2 = _top16(s2, iota_k)
    v1 = _stack_rows(vals1, iota_a)
    v2 = _stack_rows(vals2, iota_a)
    plen = jnp.zeros((PEER_TOPK, tb), F32)
    front = v1 + vals2[0]
    for _ in range(PEER_TOPK):
        m = jnp.max(front, axis=0, keepdims=True)
        a_star = jnp.min(jnp.where(front == m, iota_a, float(PEER_TOPK)), axis=0, keepdims=True)
        sel = iota_a == a_star
        plen = plen + jnp.where(sel, 1.0, 0.0)
        lsel = jnp.sum(jnp.where(sel, plen, 0.0), axis=0, keepdims=True)
        v1sel = jnp.sum(jnp.where(sel, v1, 0.0), axis=0, keepdims=True)
        v2next = jnp.sum(jnp.where(iota_a == lsel, v2, 0.0), axis=0, keepdims=True)
        fnew = jnp.where(lsel >= float(PEER_TOPK), -jnp.inf, v1sel + v2next)
        front = jnp.where(sel, fnew, front)
    e1s = jnp.exp(v1 - vals1[0])
    z = jnp.zeros((1, tb), F32)
    for b in range(PEER_TOPK):
        e2b = jnp.exp(vals2[b] - vals2[0])
        z = z + e2b * jnp.sum(jnp.where(plen > float(b), e1s, 0.0), axis=0, keepdims=True)
    lc = jnp.zeros((N_KEYS, tb), F32)
    for a in range(PEER_TOPK):
        lc = jnp.where(rank1 == float(a), plen[a:a + 1, :], lc)
    r2_ref[0] = rank2
    lc_ref[0] = lc
    e1_ref[0] = jnp.exp(s1 - vals1[0]) / z
    e2_ref[0] = jnp.exp(s2 - vals2[0])


def _peer_topk(st, *, tb):
    R, T = st.shape
    tab = pl.BlockSpec((1, N_KEYS, tb), lambda t, h: (h, 0, t))
    shp = jax.ShapeDtypeStruct((PEER_HEADS, N_KEYS, T), F32)
    return pl.pallas_call(
        _peer_topk_kernel,
        grid=(T // tb, PEER_HEADS),
        in_specs=[pl.BlockSpec((2 * N_KEYS, tb), lambda t, h: (h, t))],
        out_specs=[tab, tab, tab, tab],
        out_shape=[shp, shp, shp, shp],
        compiler_params=_cparams(("parallel", "parallel")),
        name="peer_topk",
    )(st)


def _gelu(a):
    return 0.5 * a * (1.0 + lax.erf(a * (1.0 / math.sqrt(2.0))))


def _peer_dense_kernel(xn_ref, u_ref, vt_ref, r2_ref, lc_ref, e1_ref, e2_ref, out_ref, w_scr, *, tiles):
    e = pl.program_id(1)

    @pl.when(e == 0)
    def _():
        out_ref[...] = jnp.zeros(out_ref.shape, F32)

    act = _gelu(_dot_nt(u_ref[...], xn_ref[...]))
    for ii in range(tiles):
        i = e * tiles + ii
        g = jnp.zeros((N_KEYS, act.shape[1]), F32)
        for h in range(PEER_HEADS):
            lc = lc_ref[h, pl.ds(i, 1), :]
            e1 = e1_ref[h, pl.ds(i, 1), :]
            g = g + jnp.where(r2_ref[h] < lc, e2_ref[h] * e1, 0.0)
        w_scr[ii * N_KEYS:(ii + 1) * N_KEYS, :] = (g * act[ii * N_KEYS:(ii + 1) * N_KEYS, :]).astype(BF16)
    out_ref[...] += _dot(vt_ref[...], w_scr[...])


def _peer_dense(xn, u, vt, r2, lc, e1, e2, *, tb, ec):
    T, D = xn.shape
    E = u.shape[0]
    tab = pl.BlockSpec((PEER_HEADS, N_KEYS, tb), lambda t, e: (0, 0, t))
    return pl.pallas_call(
        functools.partial(_peer_dense_kernel, tiles=ec // N_KEYS),
        grid=(T // tb, E // ec),
        in_specs=[pl.BlockSpec((tb, D), lambda t, e: (t, 0)),
                  pl.BlockSpec((ec, D), lambda t, e: (e, 0)),
                  pl.BlockSpec((D, ec), lambda t, e: (0, e)),
                  tab, tab, tab, tab],
        out_specs=pl.BlockSpec((D, tb), lambda t, e: (0, t)),
        out_shape=jax.ShapeDtypeStruct((D, T), F32),
        scratch_shapes=[pltpu.VMEM((ec, tb), BF16)],
        compiler_params=_cparams(("parallel", "arbitrary")),
        name="peer_dense",
    )(xn, u, vt, r2, lc, e1, e2)


def _final_kernel(pt_ref, h1_ref, g_ref, out_ref, *, normalize):
    h2 = h1_ref[...] + pt_ref[...].T
    out_ref[...] = _rms(h2, g_ref[...]) if normalize else h2


def _final(pt, h1, g, *, tb, normalize):
    T, D = h1.shape
    return pl.pallas_call(
        functools.partial(_final_kernel, normalize=normalize),
        grid=(T // tb,),
        in_specs=[pl.BlockSpec((D, tb), lambda i: (0, i)), pl.BlockSpec((tb, D), lambda i: (i, 0)),
                  pl.BlockSpec(g.shape, lambda i: (0, 0))],
        out_specs=pl.BlockSpec((tb, D), lambda i: (i, 0)),
        out_shape=jax.ShapeDtypeStruct((T, D), F32),
        compiler_params=_cparams(("parallel",)),
        name="final_norm",
    )(pt, h1, g)


def _rot_cols(w):
    half = w.shape[-1] // 2
    return jnp.concatenate([-w[..., half:], w[..., :half]], axis=-1)


def _layer(h2d, pos2, invf, p, *, B, S):
    T, D = h2d.shape
    w_in = p["w_in"]
    o0, o1, o2, o3 = D, 2 * D, 2 * D + Q_LORA, 2 * D + Q_LORA + KV_LORA + QK_ROPE
    w_kr = w_in[:, o2 + KV_LORA:o3]
    zpad = jnp.zeros((D, LANE - QK_ROPE), F32)
    wkv = jnp.concatenate([w_in[:, o2:o2 + KV_LORA], w_kr, zpad, _rot_cols(w_kr), zpad], axis=1)
    hglu, cq, ckv, gates = _in_proj(
        h2d, p["mix_norm_g"][None, :], w_in[:, :o0].astype(BF16), w_in[:, o0:o1].astype(BF16),
        w_in[:, o1:o2].astype(BF16), wkv.astype(BF16), w_in[:, o3:].astype(BF16), p["b_gate"][None, :], tm=512)

    w_uq = p["w_uq"].reshape(Q_LORA, N_HEADS, QK_NOPE + QK_ROPE)
    zq = jnp.zeros((Q_LORA, N_HEADS, LANE - QK_ROPE), F32)
    wqa = jnp.concatenate([w_uq, zq], axis=-1).reshape(Q_LORA, N_HEADS * QK_PAD)
    wqr = jnp.concatenate([_rot_cols(w_uq[..., QK_NOPE:]), zq], axis=-1).reshape(Q_LORA, N_HEADS * LANE)
    w_ukv = p["w_ukv"].reshape(KV_LORA, N_HEADS, QK_NOPE + V_DIM)
    wk = w_ukv[..., :QK_NOPE].reshape(KV_LORA, N_HEADS * QK_NOPE)
    wv = w_ukv[..., QK_NOPE:].reshape(KV_LORA, N_HEADS * V_DIM)
    q, k, v = _mla_proj(cq, ckv, pos2, invf, p["q_norm_g"][None, :], wqa.astype(BF16), wqr.astype(BF16),
                        p["kv_norm_g"][None, :], wk.astype(BF16), wv.astype(BF16), B=B, S=S, tm=512)
    o = _flash_attn(q, k, v, bq=512, bk=512)

    h1 = _mix(hglu, o.reshape(T, D), gates, h2d, p["w_dw"].reshape(CONV_WIDTH, D), p["b_dw"][None, :],
              p["conv_ln_g"][None, :], p["conv_ln_b"][None, :], p["w_conv_out"].astype(BF16),
              p["w_attn_out"].astype(BF16), p["w_o"].astype(BF16), S=S, tm=256)

    sk = p["peer_sub_keys"].reshape(2 * PEER_HEADS, N_KEYS, -1).astype(BF16)
    xn, st = _peer_prep(h1, p["ffn_norm_g"][None, :], p["w_peer_q"].T.astype(BF16), sk, tm=512)
    r2, lc, e1, e2 = _peer_topk(st, tb=256)
    pt = _peer_dense(xn, p["peer_u"].astype(BF16), p["peer_v"].T.astype(BF16), r2, lc, e1, e2, tb=512, ec=512)
    return h1, pt


def kernel(x, positions, mix_norm_g, w_in, b_gate, w_dw, b_dw, conv_ln_g, conv_ln_b, w_conv_out, q_norm_g, w_uq,
           kv_norm_g, w_ukv, w_attn_out, w_o, ffn_norm_g, w_peer_q, peer_sub_keys, peer_u, peer_v, final_norm_g):
    B, S, D = x.shape
    T = B * S
    stacked = dict(mix_norm_g=mix_norm_g, w_in=w_in, b_gate=b_gate, w_dw=w_dw, b_dw=b_dw, conv_ln_g=conv_ln_g,
                   conv_ln_b=conv_ln_b, w_conv_out=w_conv_out, q_norm_g=q_norm_g, w_uq=w_uq, kv_norm_g=kv_norm_g,
                   w_ukv=w_ukv, w_attn_out=w_attn_out, w_o=w_o, ffn_norm_g=ffn_norm_g, w_peer_q=w_peer_q,
                   peer_sub_keys=peer_sub_keys, peer_u=peer_u, peer_v=peer_v)
    depth = w_in.shape[0]
    half = QK_ROPE // 2
    inv_freq = ROPE_BASE ** (-jnp.arange(half, dtype=F32) / half)
    invf = jnp.concatenate([inv_freq, inv_freq, jnp.zeros((LANE - QK_ROPE,), F32)])[None, :]
    pos2 = positions.reshape(T, 1)
    h = x.reshape(T, D)
    for l in range(depth):
        p = {name: a[l] for name, a in stacked.items()}
        h1, pt = _layer(h, pos2, invf, p, B=B, S=S)
        last = l + 1 == depth
        h = _final(pt, h1, final_norm_g[None, :], tb=512, normalize=last)
    return h.reshape(B, S, D)
```

```python
import functools
import math

import jax
import jax.numpy as jnp
from jax import lax
from jax.experimental import pallas as pl
from jax.experimental.pallas import tpu as pltpu

F32 = jnp.float32
BF16 = jnp.bfloat16

EPS = 1e-6
CONV_WIDTH = 31
N_HEADS = 8
QK_NOPE = 128
QK_ROPE = 64
V_DIM = 128
Q_LORA = 256
KV_LORA = 128
ROPE_BASE = 10000.0
N_KEYS = 128
PEER_HEADS = 8
PEER_TOPK = 16
QK_PAD = 256
LANE = 128
BF16_ROWS = 16
HALO = 32
VMEM_LIMIT = 56 * 1024 * 1024


def _cparams(sem):
    return pltpu.CompilerParams(dimension_semantics=sem, vmem_limit_bytes=VMEM_LIMIT)


def _dot(a, b):
    return jnp.dot(a, b, preferred_element_type=F32)


def _dot_nt(a, b):
    return lax.dot_general(a, b, (((1,), (1,)), ((), ())), preferred_element_type=F32)


def _sigmoid(x):
    return 1.0 / (1.0 + jnp.exp(-x))


def _rms(x, g):
    return x * lax.rsqrt(jnp.mean(x * x, axis=-1, keepdims=True) + EPS) * g


def _inproj_kernel(x_ref, g_ref, wv_ref, wg_ref, wq_ref, wkv_ref, wgl_ref, bg_ref,
                   hglu_ref, cq_ref, ckv_ref, gates_ref):
    a = _rms(x_ref[...], g_ref[...]).astype(BF16)
    val = _dot(a, wv_ref[...])
    gate = _dot(a, wg_ref[...])
    hglu_ref[...] = val * _sigmoid(gate)
    cq_ref[...] = _dot(a, wq_ref[...])
    ckv_ref[...] = _dot(a, wkv_ref[...])
    gates_ref[...] = _sigmoid(_dot(a, wgl_ref[...]) + bg_ref[...]).astype(BF16)


def _in_proj(x2, g, wv, wg, wq, wkv, wgl, bg, *, tm):
    T, D = x2.shape
    full = lambda a: pl.BlockSpec(a.shape, lambda i: (0, 0))
    row = lambda n: pl.BlockSpec((tm, n), lambda i: (i, 0))
    return pl.pallas_call(
        _inproj_kernel,
        grid=(T // tm,),
        in_specs=[row(D), full(g), full(wv), full(wg), full(wq), full(wkv), full(wgl), full(bg)],
        out_specs=[row(wv.shape[1]), row(wq.shape[1]), row(wkv.shape[1]), row(wgl.shape[1])],
        out_shape=[jax.ShapeDtypeStruct((T, wv.shape[1]), F32),
                   jax.ShapeDtypeStruct((T, wq.shape[1]), F32),
                   jax.ShapeDtypeStruct((T, wkv.shape[1]), F32),
                   jax.ShapeDtypeStruct((T, wgl.shape[1]), BF16)],
        compiler_params=_cparams(("parallel",)),
        name="in_proj",
    )(x2, g, wv, wg, wq, wkv, wgl, bg)


def _mla_proj_kernel(cq_ref, ckv_ref, pos_ref, invf_ref, qg_ref, wqa_ref, wqr_ref, kvg_ref, wk_ref, wv_ref,
                     q_ref, k_ref, v_ref, *, scale):
    qn = _rms(cq_ref[...], qg_ref[...]).astype(BF16)
    ckv_full = ckv_ref[...]
    kvn = _rms(ckv_full[:, :KV_LORA], kvg_ref[...]).astype(BF16)
    ang = pos_ref[...].astype(F32) * invf_ref[...]
    cosv = jnp.cos(ang)
    sinv = jnp.sin(ang)
    kr = (ckv_full[:, KV_LORA:KV_LORA + LANE] * cosv + ckv_full[:, KV_LORA + LANE:] * sinv).astype(BF16)
    qa = _dot(qn, wqa_ref[...])
    qr = _dot(qn, wqr_ref[...])
    kn = _dot(kvn, wk_ref[...])
    vt = _dot_nt(wv_ref[...], kvn)
    for h in range(N_HEADS):
        q_ref[0, h, :, 0:LANE] = (qa[:, h * QK_PAD:h * QK_PAD + LANE] * scale).astype(BF16)
        q_rope = qa[:, h * QK_PAD + LANE:(h + 1) * QK_PAD] * cosv + qr[:, h * LANE:(h + 1) * LANE] * sinv
        q_ref[0, h, :, LANE:QK_PAD] = (q_rope * scale).astype(BF16)
        k_ref[0, h, :, 0:LANE] = kn[:, h * LANE:(h + 1) * LANE].astype(BF16)
        k_ref[0, h, :, LANE:QK_PAD] = kr
        v_ref[0, h, :, :] = vt[h * V_DIM:(h + 1) * V_DIM, :].astype(BF16)


def _mla_proj(cq, ckv, pos2, invf, qg, wqa, wqr, kvg, wk, wv, *, B, S, tm):
    T = cq.shape[0]
    nb = S // tm
    full = lambda a: pl.BlockSpec(a.shape, lambda i: (0, 0))
    row = lambda n: pl.BlockSpec((tm, n), lambda i: (i, 0))
    head = lambda n: pl.BlockSpec((1, N_HEADS, tm, n), lambda i: (i // nb, 0, i % nb, 0))
    head_t = pl.BlockSpec((1, N_HEADS, V_DIM, tm), lambda i: (i // nb, 0, 0, i % nb))
    scale = math.log2(math.e) / math.sqrt(QK_NOPE + QK_ROPE)
    return pl.pallas_call(
        functools.partial(_mla_proj_kernel, scale=scale),
        grid=(T // tm,),
        in_specs=[row(cq.shape[1]), row(ckv.shape[1]), row(1), full(invf), full(qg), full(wqa), full(wqr),
                  full(kvg), full(wk), full(wv)],
        out_specs=[head(QK_PAD), head(QK_PAD), head_t],
        out_shape=[jax.ShapeDtypeStruct((B, N_HEADS, S, QK_PAD), BF16),
                   jax.ShapeDtypeStruct((B, N_HEADS, S, QK_PAD), BF16),
                   jax.ShapeDtypeStruct((B, N_HEADS, V_DIM, S), BF16)],
        compiler_params=_cparams(("parallel",)),
        name="mla_proj",
    )(cq, ckv, pos2, invf, qg, wqa, wqr, kvg, wk, wv)


def _flash_kernel(q_ref, k_ref, vt_ref, o_ref, acc_scr, *, bq, bk, heads):
    qi = pl.program_id(2)
    acc_scr[...] = jnp.zeros(acc_scr.shape, F32)

    def step(ki, carry, diagonal):
        off = pl.multiple_of(ki * bk, bk)
        out = []
        for hh in range(heads):
            m_prev, l_prev = carry[2 * hh], carry[2 * hh + 1]
            st = _dot_nt(k_ref[0, hh, pl.ds(off, bk), :], q_ref[0, hh])
            if diagonal:
                kpos = lax.broadcasted_iota(jnp.int32, (bk, bq), 0)
                qpos = lax.broadcasted_iota(jnp.int32, (bk, bq), 1)
                st = jnp.where(kpos <= qpos, st, -1e30)
            m_new = jnp.maximum(m_prev, jnp.max(st, axis=0, keepdims=True))
            p = jnp.exp2(st - m_new)
            alpha = jnp.exp2(m_prev - m_new)
            l_new = alpha * l_prev + jnp.sum(p, axis=0, keepdims=True)
            pv = _dot(vt_ref[0, hh, :, pl.ds(off, bk)], p.astype(BF16))
            acc_scr[hh] = alpha * acc_scr[hh] + pv
            out += [m_new, l_new]
        return tuple(out)

    init = (jnp.full((1, bq), -1e30, F32), jnp.zeros((1, bq), F32)) * heads
    carry = lax.fori_loop(0, qi, lambda ki, c: step(ki, c, False), init)
    carry = step(qi, carry, True)
    for hh in range(heads):
        o_ref[0, :, hh * V_DIM:(hh + 1) * V_DIM] = (acc_scr[hh] / carry[2 * hh + 1]).T.astype(BF16)


def _flash_attn(q, k, vt, *, bq, heads):
    B, H, S, _ = q.shape
    return pl.pallas_call(
        functools.partial(_flash_kernel, bq=bq, bk=bq, heads=heads),
        grid=(B, H // heads, S // bq),
        in_specs=[pl.BlockSpec((1, heads, bq, QK_PAD), lambda b, h, i: (b, h, i, 0)),
                  pl.BlockSpec((1, heads, S, QK_PAD), lambda b, h, i: (b, h, 0, 0)),
                  pl.BlockSpec((1, heads, V_DIM, S), lambda b, h, i: (b, h, 0, 0))],
        out_specs=pl.BlockSpec((1, bq, heads * V_DIM), lambda b, h, i: (b, i, h)),
        out_shape=jax.ShapeDtypeStruct((B, S, H * V_DIM), BF16),
        scratch_shapes=[pltpu.VMEM((heads, V_DIM, bq), F32)],
        compiler_params=_cparams(("parallel", "parallel", "arbitrary")),
        name="flash_attn",
    )(q, k, vt)


def _mix_kernel(hc_ref, hp_ref, o_ref, gates_ref, x_ref, wdw_ref, bdw_ref, lng_ref, lnb_ref,
                wco_ref, wao_ref, wo_ref, h1_ref, hbuf, cbuf, *, tm, blocks_per_seq, rows):
    i = pl.program_id(0)
    first = (i % blocks_per_seq) == 0
    hbuf[0:HALO, :] = jnp.where(first, 0.0, hp_ref[...])
    hbuf[HALO:HALO + tm, :] = hc_ref[...]
    D = hc_ref.shape[1]
    base = HALO - (CONV_WIDTH - 1)
    for c in range(D // LANE):
        cs = slice(c * LANE, (c + 1) * LANE)
        for r in range(tm // rows):
            acc = jnp.broadcast_to(bdw_ref[:, cs], (rows, LANE))
            for kk in range(CONV_WIDTH):
                start = r * rows + base + kk
                acc = acc + wdw_ref[kk:kk + 1, cs] * hbuf[start:start + rows, cs]
            cbuf[r * rows:(r + 1) * rows, cs] = acc
    y = cbuf[...]
    mu = jnp.mean(y, axis=-1, keepdims=True)
    yc = y - mu
    var = jnp.mean(yc * yc, axis=-1, keepdims=True)
    yn = yc * lax.rsqrt(var + EPS) * lng_ref[...] + lnb_ref[...]
    act = (yn * _sigmoid(yn)).astype(BF16)
    y_conv = _dot(act, wco_ref[...])
    y_attn = _dot(o_ref[...], wao_ref[...])
    gts = gates_ref[...].astype(F32)
    mixed = (gts[:, :D] * y_conv + gts[:, D:] * y_attn).astype(BF16)
    h1_ref[...] = x_ref[...] + _dot(mixed, wo_ref[...])


def _mix(hglu, o2, gates, x2, wdw, bdw, lng, lnb, wco, wao, wo, *, S, tm):
    T, D = x2.shape
    full = lambda a: pl.BlockSpec(a.shape, lambda i: (0, 0))
    row = lambda n: pl.BlockSpec((tm, n), lambda i: (i, 0))
    per = tm // HALO
    halo = pl.BlockSpec((HALO, D), lambda i: (jnp.maximum(i * per - 1, 0), 0))
    return pl.pallas_call(
        functools.partial(_mix_kernel, tm=tm, blocks_per_seq=S // tm, rows=64),
        grid=(T // tm,),
        in_specs=[row(D), halo, row(D), row(2 * D), row(D), full(wdw), full(bdw), full(lng), full(lnb),
                  full(wco), full(wao), full(wo)],
        out_specs=row(D),
        out_shape=jax.ShapeDtypeStruct((T, D), F32),
        scratch_shapes=[pltpu.VMEM((HALO + tm, D), F32), pltpu.VMEM((tm, D), F32)],
        compiler_params=_cparams(("parallel",)),
        name="mix",
    )(hglu, hglu, o2, gates, x2, wdw, bdw, lng, lnb, wco, wao, wo)


def _peer_prep_kernel(h1_ref, g_ref, wpqt_ref, sk_ref, xn_ref, st_ref):
    xn = _rms(h1_ref[...], g_ref[...]).astype(BF16)
    xn_ref[...] = xn
    qt = _dot_nt(wpqt_ref[...], xn).astype(BF16)
    for hp in range(2 * PEER_HEADS):
        rs = slice(hp * N_KEYS, (hp + 1) * N_KEYS)
        st_ref[rs, :] = _dot(sk_ref[hp], qt[rs, :])


def _peer_prep(h1, g, wpqt, sk, *, tm):
    T, D = h1.shape
    R = wpqt.shape[0]
    return pl.pallas_call(
        _peer_prep_kernel,
        grid=(T // tm,),
        in_specs=[pl.BlockSpec((tm, D), lambda i: (i, 0)), pl.BlockSpec(g.shape, lambda i: (0, 0)),
                  pl.BlockSpec(wpqt.shape, lambda i: (0, 0)), pl.BlockSpec(sk.shape, lambda i: (0, 0, 0))],
        out_specs=[pl.BlockSpec((tm, D), lambda i: (i, 0)), pl.BlockSpec((R, tm), lambda i: (0, i))],
        out_shape=[jax.ShapeDtypeStruct((T, D), BF16), jax.ShapeDtypeStruct((R, T), F32)],
        compiler_params=_cparams(("parallel",)),
        name="peer_prep",
    )(h1, g, wpqt, sk)


def _top16(s, iota_k):
    work = s
    rank = jnp.full(s.shape, float(PEER_TOPK), F32)
    vals = []
    for a in range(PEER_TOPK):
        m = jnp.max(work, axis=0, keepdims=True)
        idx = jnp.min(jnp.where(work == m, iota_k, float(N_KEYS)), axis=0, keepdims=True)
        sel = iota_k == idx
        rank = jnp.where(sel, float(a), rank)
        work = jnp.where(sel, -jnp.inf, work)
        vals.append(m)
    return vals, rank


def _stack_rows(rows, iota_a):
    out = jnp.zeros(iota_a.shape, F32)
    for a, r in enumerate(rows):
        out = jnp.where(iota_a == float(a), r, out)
    return out


def _peer_topk_kernel(st_ref, r2_ref, lc_ref, e1_ref, e2_ref):
    tb = st_ref.shape[1]
    s1 = st_ref[0:N_KEYS, :]
    s2 = st_ref[N_KEYS:2 * N_KEYS, :]
    iota_k = lax.broadcasted_iota(jnp.int32, (N_KEYS, tb), 0).astype(F32)
    iota_a = lax.broadcasted_iota(jnp.int32, (PEER_TOPK, tb), 0).astype(F32)
    vals1, rank1 = _top16(s1, iota_k)
    vals2, rank2 = _top16(s2, iota_k)
    v1 = _stack_rows(vals1, iota_a)
    v2 = _stack_rows(vals2, iota_a)
    plen = jnp.zeros((PEER_TOPK, tb), F32)
    front = v1 + vals2[0]
    for _ in range(PEER_TOPK):
        m = jnp.max(front, axis=0, keepdims=True)
        a_star = jnp.min(jnp.where(front == m, iota_a, float(PEER_TOPK)), axis=0, keepdims=True)
        sel = iota_a == a_star
        plen = plen + jnp.where(sel, 1.0, 0.0)
        lsel = jnp.sum(jnp.where(sel, plen, 0.0), axis=0, keepdims=True)
        v1sel = jnp.sum(jnp.where(sel, v1, 0.0), axis=0, keepdims=True)
        v2next = jnp.sum(jnp.where(iota_a == lsel, v2, 0.0), axis=0, keepdims=True)
        fnew = jnp.where(lsel >= float(PEER_TOPK), -jnp.inf, v1sel + v2next)
        front = jnp.where(sel, fnew, front)
    e1s = jnp.exp(v1 - vals1[0])
    z = jnp.zeros((1, tb), F32)
    for b in range(PEER_TOPK):
        e2b = jnp.exp(vals2[b] - vals2[0])
        z = z + e2b * jnp.sum(jnp.where(plen > float(b), e1s, 0.0), axis=0, keepdims=True)
    lc = jnp.zeros((N_KEYS, tb), F32)
    for a in range(PEER_TOPK):
        lc = jnp.where(rank1 == float(a), plen[a:a + 1, :], lc)
    r2_ref[0] = rank2
    lc_ref[0] = lc
    e1_ref[0] = jnp.exp(s1 - vals1[0]) / z
    e2_ref[0] = jnp.exp(s2 - vals2[0])


def _peer_topk(st, *, tb):
    R, T = st.shape
    tab = pl.BlockSpec((1, N_KEYS, tb), lambda t, h: (h, 0, t))
    shp = jax.ShapeDtypeStruct((PEER_HEADS, N_KEYS, T), F32)
    return pl.pallas_call(
        _peer_topk_kernel,
        grid=(T // tb, PEER_HEADS),
        in_specs=[pl.BlockSpec((2 * N_KEYS, tb), lambda t, h: (h, t))],
        out_specs=[tab, tab, tab, tab],
        out_shape=[shp, shp, shp, shp],
        compiler_params=_cparams(("parallel", "parallel")),
        name="peer_topk",
    )(st)


def _gelu(a):
    return 0.5 * a * (1.0 + lax.erf(a * (1.0 / math.sqrt(2.0))))


def _peer_dense_kernel(xn_ref, u_ref, vt_ref, r2_ref, lc_ref, e1_ref, e2_ref, out_ref, r2b, e2b, w_scr, act_scr,
                       *, sub, nsub):
    e = pl.program_id(1)
    tb = xn_ref.shape[0]
    packed = (PEER_HEADS, N_KEYS // BF16_ROWS, BF16_ROWS, tb)

    @pl.when(e == 0)
    def _():
        out_ref[...] = jnp.zeros(out_ref.shape, F32)
        r2b[...] = r2_ref[...].reshape(packed).astype(BF16)
        e2b[...] = e2_ref[...].reshape(packed).astype(BF16)

    xn = xn_ref[...]
    tiles = sub // N_KEYS
    zero = jnp.zeros((), BF16)
    for s in range(nsub):
        rows = slice(s * sub, (s + 1) * sub)
        act_scr[s] = _dot_nt(u_ref[rows, :], xn)
        for ii in range(tiles):
            i = s * tiles + ii
            for c in range(tb // LANE):
                cs = slice(c * LANE, (c + 1) * LANE)
                g = jnp.zeros((N_KEYS // BF16_ROWS, BF16_ROWS, LANE), BF16)
                for h in range(PEER_HEADS):
                    lc = jnp.broadcast_to(lc_ref[h, i, :, cs], (BF16_ROWS, LANE)).astype(BF16)
                    e1 = jnp.broadcast_to(e1_ref[h, i, :, cs], (BF16_ROWS, LANE)).astype(BF16)
                    g = g + jnp.where(r2b[h, :, :, cs] < lc[None], e2b[h, :, :, cs] * e1[None], zero)
                a = _gelu(act_scr[s, ii * N_KEYS:(ii + 1) * N_KEYS, cs]).astype(BF16)
                w_scr[s * sub + ii * N_KEYS:s * sub + (ii + 1) * N_KEYS, cs] = g.reshape(N_KEYS, LANE) * a
        out_ref[...] += _dot(vt_ref[:, rows], w_scr[rows, :])


def _peer_dense(xn, u, vt, r2, lc, e1, e2, *, tb, sub, nsub):
    T, D = xn.shape
    E = u.shape[0]
    ec = sub * nsub
    tab = pl.BlockSpec((PEER_HEADS, N_KEYS, tb), lambda t, e: (0, 0, t))
    rows = pl.BlockSpec((PEER_HEADS, ec // N_KEYS, 1, tb), lambda t, e: (0, e, 0, t))
    lc = lc.reshape(PEER_HEADS, N_KEYS, 1, T)
    e1 = e1.reshape(PEER_HEADS, N_KEYS, 1, T)
    packed = (PEER_HEADS, N_KEYS // BF16_ROWS, BF16_ROWS, tb)
    return pl.pallas_call(
        functools.partial(_peer_dense_kernel, sub=sub, nsub=nsub),
        grid=(T // tb, E // ec),
        in_specs=[pl.BlockSpec((tb, D), lambda t, e: (t, 0)),
                  pl.BlockSpec((ec, D), lambda t, e: (e, 0)),
                  pl.BlockSpec((D, ec), lambda t, e: (0, e)),
                  tab, rows, rows, tab],
        out_specs=pl.BlockSpec((D, tb), lambda t, e: (0, t)),
        out_shape=jax.ShapeDtypeStruct((D, T), F32),
        scratch_shapes=[pltpu.VMEM(packed, BF16), pltpu.VMEM(packed, BF16), pltpu.VMEM((ec, tb), BF16),
                        pltpu.VMEM((nsub, sub, tb), F32)],
        compiler_params=_cparams(("parallel", "arbitrary")),
        name="peer_dense",
    )(xn, u, vt, r2, lc, e1, e2)


def _final_kernel(pt_ref, h1_ref, g_ref, out_ref, *, normalize):
    h2 = h1_ref[...] + pt_ref[...].T
    out_ref[...] = _rms(h2, g_ref[...]) if normalize else h2


def _final(pt, h1, g, *, tb, normalize):
    T, D = h1.shape
    return pl.pallas_call(
        functools.partial(_final_kernel, normalize=normalize),
        grid=(T // tb,),
        in_specs=[pl.BlockSpec((D, tb), lambda i: (0, i)), pl.BlockSpec((tb, D), lambda i: (i, 0)),
                  pl.BlockSpec(g.shape, lambda i: (0, 0))],
        out_specs=pl.BlockSpec((tb, D), lambda i: (i, 0)),
        out_shape=jax.ShapeDtypeStruct((T, D), F32),
        compiler_params=_cparams(("parallel",)),
        name="final_norm",
    )(pt, h1, g)


def _rot_cols(w):
    half = w.shape[-1] // 2
    return jnp.concatenate([-w[..., half:], w[..., :half]], axis=-1)


def _layer(h2d, pos2, invf, p, *, B, S):
    T, D = h2d.shape
    w_in = p["w_in"]
    o0, o1, o2, o3 = D, 2 * D, 2 * D + Q_LORA, 2 * D + Q_LORA + KV_LORA + QK_ROPE
    w_kr = w_in[:, o2 + KV_LORA:o3]
    zpad = jnp.zeros((D, LANE - QK_ROPE), F32)
    wkv = jnp.concatenate([w_in[:, o2:o2 + KV_LORA], w_kr, zpad, _rot_cols(w_kr), zpad], axis=1)
    hglu, cq, ckv, gates = _in_proj(
        h2d, p["mix_norm_g"][None, :], w_in[:, :o0].astype(BF16), w_in[:, o0:o1].astype(BF16),
        w_in[:, o1:o2].astype(BF16), wkv.astype(BF16), w_in[:, o3:].astype(BF16), p["b_gate"][None, :], tm=512)

    w_uq = p["w_uq"].reshape(Q_LORA, N_HEADS, QK_NOPE + QK_ROPE)
    zq = jnp.zeros((Q_LORA, N_HEADS, LANE - QK_ROPE), F32)
    wqa = jnp.concatenate([w_uq, zq], axis=-1).reshape(Q_LORA, N_HEADS * QK_PAD)
    wqr = jnp.concatenate([_rot_cols(w_uq[..., QK_NOPE:]), zq], axis=-1).reshape(Q_LORA, N_HEADS * LANE)
    w_ukv = p["w_ukv"].reshape(KV_LORA, N_HEADS, QK_NOPE + V_DIM)
    wk = w_ukv[..., :QK_NOPE].reshape(KV_LORA, N_HEADS * QK_NOPE)
    wv = w_ukv[..., QK_NOPE:].reshape(KV_LORA, N_HEADS * V_DIM)
    q, k, v = _mla_proj(cq, ckv, pos2, invf, p["q_norm_g"][None, :], wqa.astype(BF16), wqr.astype(BF16),
                        p["kv_norm_g"][None, :], wk.astype(BF16), wv.T.astype(BF16), B=B, S=S, tm=512)
    o = _flash_attn(q, k, v, bq=512, heads=2)

    h1 = _mix(hglu, o.reshape(T, D), gates, h2d, p["w_dw"].reshape(CONV_WIDTH, D), p["b_dw"][None, :],
              p["conv_ln_g"][None, :], p["conv_ln_b"][None, :], p["w_conv_out"].astype(BF16),
              p["w_attn_out"].astype(BF16), p["w_o"].astype(BF16), S=S, tm=256)

    sk = p["peer_sub_keys"].reshape(2 * PEER_HEADS, N_KEYS, -1).astype(BF16)
    xn, st = _peer_prep(h1, p["ffn_norm_g"][None, :], p["w_peer_q"].T.astype(BF16), sk, tm=512)
    r2, lc, e1, e2 = _peer_topk(st, tb=256)
    pt = _peer_dense(xn, p["peer_u"].astype(BF16), p["peer_v"].T.astype(BF16), r2, lc, e1, e2, tb=512, sub=512, nsub=2)
    return h1, pt


def kernel(x, positions, mix_norm_g, w_in, b_gate, w_dw, b_dw, conv_ln_g, conv_ln_b, w_conv_out, q_norm_g, w_uq,
           kv_norm_g, w_ukv, w_attn_out, w_o, ffn_norm_g, w_peer_q, peer_sub_keys, peer_u, peer_v, final_norm_g):
    B, S, D = x.shape
    T = B * S
    stacked = dict(mix_norm_g=mix_norm_g, w_in=w_in, b_gate=b_gate, w_dw=w_dw, b_dw=b_dw, conv_ln_g=conv_ln_g,
                   conv_ln_b=conv_ln_b, w_conv_out=w_conv_out, q_norm_g=q_norm_g, w_uq=w_uq, kv_norm_g=kv_norm_g,
                   w_ukv=w_ukv, w_attn_out=w_attn_out, w_o=w_o, ffn_norm_g=ffn_norm_g, w_peer_q=w_peer_q,
                   peer_sub_keys=peer_sub_keys, peer_u=peer_u, peer_v=peer_v)
    depth = w_in.shape[0]
    half = QK_ROPE // 2
    inv_freq = ROPE_BASE ** (-jnp.arange(half, dtype=F32) / half)
    invf = jnp.concatenate([inv_freq, inv_freq, jnp.zeros((LANE - QK_ROPE,), F32)])[None, :]
    pos2 = positions.reshape(T, 1)
    h = x.reshape(T, D)
    for l in range(depth):
        p = {name: a[l] for name, a in stacked.items()}
        h1, pt = _layer(h, pos2, invf, p, B=B, S=S)
        last = l + 1 == depth
        h = _final(pt, h1, final_norm_g[None, :], tb=512, normalize=last)
    return h.reshape(B, S, D)
```

```python
import functools
import math

import jax
import jax.numpy as jnp
from jax import lax
from jax.experimental import pallas as pl
from jax.experimental.pallas import tpu as pltpu

F32 = jnp.float32
BF16 = jnp.bfloat16

EPS = 1e-6
CONV_WIDTH = 31
N_HEADS = 8
QK_NOPE = 128
QK_ROPE = 64
V_DIM = 128
Q_LORA = 256
KV_LORA = 128
ROPE_BASE = 10000.0
N_KEYS = 128
PEER_HEADS = 8
PEER_TOPK = 16
QK_PAD = 256
LANE = 128
SUB = 8
BF16_ROWS = 16
HALO = 32
VMEM_LIMIT = 56 * 1024 * 1024


def _cparams(sem):
    return pltpu.CompilerParams(dimension_semantics=sem, vmem_limit_bytes=VMEM_LIMIT)


def _dot(a, b):
    return jnp.dot(a, b, preferred_element_type=F32)


def _dot_nt(a, b):
    return lax.dot_general(a, b, (((1,), (1,)), ((), ())), preferred_element_type=F32)


def _sigmoid(x):
    return 1.0 / (1.0 + jnp.exp(-x))


def _rms(x, g):
    return x * lax.rsqrt(jnp.mean(x * x, axis=-1, keepdims=True) + EPS) * g


def _inproj_kernel(x_ref, g_ref, wv_ref, wg_ref, wq_ref, wkv_ref, wgl_ref, bg_ref,
                   hglu_ref, cq_ref, ckv_ref, gates_ref):
    a = _rms(x_ref[...], g_ref[...]).astype(BF16)
    val = _dot(a, wv_ref[...])
    gate = _dot(a, wg_ref[...])
    hglu_ref[...] = val * _sigmoid(gate)
    cq_ref[...] = _dot(a, wq_ref[...])
    ckv_ref[...] = _dot(a, wkv_ref[...])
    gates_ref[...] = _sigmoid(_dot(a, wgl_ref[...]) + bg_ref[...]).astype(BF16)


def _in_proj(x2, g, wv, wg, wq, wkv, wgl, bg, *, tm):
    T, D = x2.shape
    full = lambda a: pl.BlockSpec(a.shape, lambda i: (0, 0))
    row = lambda n: pl.BlockSpec((tm, n), lambda i: (i, 0))
    return pl.pallas_call(
        _inproj_kernel,
        grid=(T // tm,),
        in_specs=[row(D), full(g), full(wv), full(wg), full(wq), full(wkv), full(wgl), full(bg)],
        out_specs=[row(wv.shape[1]), row(wq.shape[1]), row(wkv.shape[1]), row(wgl.shape[1])],
        out_shape=[jax.ShapeDtypeStruct((T, wv.shape[1]), F32),
                   jax.ShapeDtypeStruct((T, wq.shape[1]), F32),
                   jax.ShapeDtypeStruct((T, wkv.shape[1]), F32),
                   jax.ShapeDtypeStruct((T, wgl.shape[1]), BF16)],
        compiler_params=_cparams(("parallel",)),
        name="in_proj",
    )(x2, g, wv, wg, wq, wkv, wgl, bg)


def _mla_proj_kernel(cq_ref, ckv_ref, pos_ref, invf_ref, qg_ref, wqa_ref, wqr_ref, kvg_ref, wk_ref, wv_ref,
                     q_ref, k_ref, v_ref, *, scale):
    qn = _rms(cq_ref[...], qg_ref[...]).astype(BF16)
    ckv_full = ckv_ref[...]
    kvn = _rms(ckv_full[:, :KV_LORA], kvg_ref[...]).astype(BF16)
    ang = pos_ref[...].astype(F32) * invf_ref[...]
    cosv = jnp.cos(ang)
    sinv = jnp.sin(ang)
    kr = (ckv_full[:, KV_LORA:KV_LORA + LANE] * cosv + ckv_full[:, KV_LORA + LANE:] * sinv).astype(BF16)
    qa = _dot(qn, wqa_ref[...])
    qr = _dot(qn, wqr_ref[...])
    kn = _dot(kvn, wk_ref[...])
    vt = _dot_nt(wv_ref[...], kvn)
    for h in range(N_HEADS):
        q_ref[0, h, :, 0:LANE] = (qa[:, h * QK_PAD:h * QK_PAD + LANE] * scale).astype(BF16)
        q_rope = qa[:, h * QK_PAD + LANE:(h + 1) * QK_PAD] * cosv + qr[:, h * LANE:(h + 1) * LANE] * sinv
        q_ref[0, h, :, LANE:QK_PAD] = (q_rope * scale).astype(BF16)
        k_ref[0, h, :, 0:LANE] = kn[:, h * LANE:(h + 1) * LANE].astype(BF16)
        k_ref[0, h, :, LANE:QK_PAD] = kr
        v_ref[0, h, :, :] = vt[h * V_DIM:(h + 1) * V_DIM, :].astype(BF16)


def _mla_proj(cq, ckv, pos2, invf, qg, wqa, wqr, kvg, wk, wv, *, B, S, tm):
    T = cq.shape[0]
    nb = S // tm
    full = lambda a: pl.BlockSpec(a.shape, lambda i: (0, 0))
    row = lambda n: pl.BlockSpec((tm, n), lambda i: (i, 0))
    head = lambda n: pl.BlockSpec((1, N_HEADS, tm, n), lambda i: (i // nb, 0, i % nb, 0))
    head_t = pl.BlockSpec((1, N_HEADS, V_DIM, tm), lambda i: (i // nb, 0, 0, i % nb))
    scale = math.log2(math.e) / math.sqrt(QK_NOPE + QK_ROPE)
    return pl.pallas_call(
        functools.partial(_mla_proj_kernel, scale=scale),
        grid=(T // tm,),
        in_specs=[row(cq.shape[1]), row(ckv.shape[1]), row(1), full(invf), full(qg), full(wqa), full(wqr),
                  full(kvg), full(wk), full(wv)],
        out_specs=[head(QK_PAD), head(QK_PAD), head_t],
        out_shape=[jax.ShapeDtypeStruct((B, N_HEADS, S, QK_PAD), BF16),
                   jax.ShapeDtypeStruct((B, N_HEADS, S, QK_PAD), BF16),
                   jax.ShapeDtypeStruct((B, N_HEADS, V_DIM, S), BF16)],
        compiler_params=_cparams(("parallel",)),
        name="mla_proj",
    )(cq, ckv, pos2, invf, qg, wqa, wqr, kvg, wk, wv)


def _flash_kernel(q_ref, k_ref, vt_ref, o_ref, acc_scr, *, bq, bk, heads):
    qi = pl.program_id(2)
    acc_scr[...] = jnp.zeros(acc_scr.shape, F32)

    def step(ki, carry, diagonal):
        off = pl.multiple_of(ki * bk, bk)
        out = []
        for hh in range(heads):
            m_prev, l_prev = carry[2 * hh], carry[2 * hh + 1]
            st = _dot_nt(k_ref[0, hh, pl.ds(off, bk), :], q_ref[0, hh])
            if diagonal:
                kpos = lax.broadcasted_iota(jnp.int32, (bk, bq), 0)
                qpos = lax.broadcasted_iota(jnp.int32, (bk, bq), 1)
                st = jnp.where(kpos <= qpos, st, -1e30)
            m_new = jnp.maximum(m_prev, jnp.max(st, axis=0, keepdims=True))
            p = jnp.exp2(st - m_new)
            alpha = jnp.exp2(m_prev - m_new)
            l_new = alpha * l_prev + jnp.sum(p, axis=0, keepdims=True)
            pv = _dot(vt_ref[0, hh, :, pl.ds(off, bk)], p.astype(BF16))
            acc_scr[hh] = alpha * acc_scr[hh] + pv
            out += [m_new, l_new]
        return tuple(out)

    init = (jnp.full((1, bq), -1e30, F32), jnp.zeros((1, bq), F32)) * heads
    carry = lax.fori_loop(0, qi, lambda ki, c: step(ki, c, False), init)
    carry = step(qi, carry, True)
    for hh in range(heads):
        o_ref[0, :, hh * V_DIM:(hh + 1) * V_DIM] = (acc_scr[hh] / carry[2 * hh + 1]).T.astype(BF16)


def _flash_attn(q, k, vt, *, bq, heads):
    B, H, S, _ = q.shape
    return pl.pallas_call(
        functools.partial(_flash_kernel, bq=bq, bk=bq, heads=heads),
        grid=(B, H // heads, S // bq),
        in_specs=[pl.BlockSpec((1, heads, bq, QK_PAD), lambda b, h, i: (b, h, i, 0)),
                  pl.BlockSpec((1, heads, S, QK_PAD), lambda b, h, i: (b, h, 0, 0)),
                  pl.BlockSpec((1, heads, V_DIM, S), lambda b, h, i: (b, h, 0, 0))],
        out_specs=pl.BlockSpec((1, bq, heads * V_DIM), lambda b, h, i: (b, i, h)),
        out_shape=jax.ShapeDtypeStruct((B, S, H * V_DIM), BF16),
        scratch_shapes=[pltpu.VMEM((heads, V_DIM, bq), F32)],
        compiler_params=_cparams(("parallel", "parallel", "arbitrary")),
        name="flash_attn",
    )(q, k, vt)


def _mix_kernel(hc_ref, hp_ref, o_ref, gates_ref, x_ref, wdw_ref, bdw_ref, lng_ref, lnb_ref,
                wco_ref, wao_ref, wo_ref, h1_ref, hbuf, cbuf, *, tm, blocks_per_seq, rows):
    i = pl.program_id(0)
    first = (i % blocks_per_seq) == 0
    hbuf[0:HALO, :] = jnp.where(first, 0.0, hp_ref[...])
    hbuf[HALO:HALO + tm, :] = hc_ref[...]
    D = hc_ref.shape[1]
    base = HALO - (CONV_WIDTH - 1)
    for c in range(D // LANE):
        cs = slice(c * LANE, (c + 1) * LANE)
        for r in range(tm // rows):
            acc = jnp.broadcast_to(bdw_ref[:, cs], (rows, LANE))
            for kk in range(CONV_WIDTH):
                start = r * rows + base + kk
                acc = acc + wdw_ref[kk:kk + 1, cs] * hbuf[start:start + rows, cs]
            cbuf[r * rows:(r + 1) * rows, cs] = acc
    y = cbuf[...]
    mu = jnp.mean(y, axis=-1, keepdims=True)
    yc = y - mu
    var = jnp.mean(yc * yc, axis=-1, keepdims=True)
    yn = yc * lax.rsqrt(var + EPS) * lng_ref[...] + lnb_ref[...]
    act = (yn * _sigmoid(yn)).astype(BF16)
    y_conv = _dot(act, wco_ref[...])
    y_attn = _dot(o_ref[...], wao_ref[...])
    gts = gates_ref[...].astype(F32)
    mixed = (gts[:, :D] * y_conv + gts[:, D:] * y_attn).astype(BF16)
    h1_ref[...] = x_ref[...] + _dot(mixed, wo_ref[...])


def _mix(hglu, o2, gates, x2, wdw, bdw, lng, lnb, wco, wao, wo, *, S, tm):
    T, D = x2.shape
    full = lambda a: pl.BlockSpec(a.shape, lambda i: (0, 0))
    row = lambda n: pl.BlockSpec((tm, n), lambda i: (i, 0))
    per = tm // HALO
    halo = pl.BlockSpec((HALO, D), lambda i: (jnp.maximum(i * per - 1, 0), 0))
    return pl.pallas_call(
        functools.partial(_mix_kernel, tm=tm, blocks_per_seq=S // tm, rows=64),
        grid=(T // tm,),
        in_specs=[row(D), halo, row(D), row(2 * D), row(D), full(wdw), full(bdw), full(lng), full(lnb),
                  full(wco), full(wao), full(wo)],
        out_specs=row(D),
        out_shape=jax.ShapeDtypeStruct((T, D), F32),
        scratch_shapes=[pltpu.VMEM((HALO + tm, D), F32), pltpu.VMEM((tm, D), F32)],
        compiler_params=_cparams(("parallel",)),
        name="mix",
    )(hglu, hglu, o2, gates, x2, wdw, bdw, lng, lnb, wco, wao, wo)


def _peer_prep_kernel(h1_ref, g_ref, wpqt_ref, sk_ref, xnt_ref, st_ref):
    tm = h1_ref.shape[0]
    xn32 = _rms(h1_ref[...], g_ref[...])
    xnt_ref[...] = xn32.T.astype(BF16)
    xn = xn32.astype(BF16)
    qt = _dot_nt(wpqt_ref[...], xn).astype(BF16)
    groups = N_KEYS // SUB
    for hp in range(2 * PEER_HEADS):
        rs = slice(hp * N_KEYS, (hp + 1) * N_KEYS)
        s = _dot(sk_ref[hp], qt[rs, :])
        for a in range(tm // LANE):
            st_ref[hp * groups:(hp + 1) * groups, a * SUB:(a + 1) * SUB, :] = (
                s[:, a * LANE:(a + 1) * LANE].reshape(groups, SUB, LANE))


def _peer_prep(h1, g, wpqt, sk, *, tm):
    T, D = h1.shape
    G = wpqt.shape[0] // SUB
    return pl.pallas_call(
        _peer_prep_kernel,
        grid=(T // tm,),
        in_specs=[pl.BlockSpec((tm, D), lambda i: (i, 0)), pl.BlockSpec(g.shape, lambda i: (0, 0)),
                  pl.BlockSpec(wpqt.shape, lambda i: (0, 0)), pl.BlockSpec(sk.shape, lambda i: (0, 0, 0))],
        out_specs=[pl.BlockSpec((D, tm), lambda i: (0, i)),
                   pl.BlockSpec((G, tm // LANE * SUB, LANE), lambda i: (0, i, 0))],
        out_shape=[jax.ShapeDtypeStruct((D, T), BF16), jax.ShapeDtypeStruct((G, T // LANE * SUB, LANE), F32)],
        compiler_params=_cparams(("parallel",)),
        name="peer_prep",
    )(h1, g, wpqt, sk)


def _tree(op, xs):
    xs = list(xs)
    while len(xs) > 1:
        xs = [op(xs[i], xs[i + 1]) for i in range(0, len(xs) - 1, 2)] + ([xs[-1]] if len(xs) % 2 else [])
    return xs[0]


def _first_index(xs, m, chains):
    n = len(xs)
    per = n // chains
    heads = []
    for c in range(chains):
        idx = jnp.full(m.shape, float(n), F32)
        for i in reversed(range(c * per, (c + 1) * per)):
            idx = jnp.where(xs[i] == m, float(i), idx)
        heads.append(idx)
    return _tree(jnp.minimum, heads)


def _peer_topk_kernel(st_ref, r2_ref, lc_ref, e1_ref, e2_ref, work, rank2, vals, idx1):
    K = PEER_TOPK

    def rows(k):
        return k // SUB, pl.ds(k % SUB, SUB, stride=SUB)

    def score(p, k):
        g, r = rows(p * N_KEYS + k)
        return st_ref[g, r, :]

    for p in range(2):
        for k in range(N_KEYS):
            work[p, k] = score(p, k)
    for k in range(N_KEYS):
        rank2[k] = jnp.full((SUB, LANE), float(K), F32)

    def extract(a, carry):
        af = a.astype(F32)
        for p in range(2):
            w = [work[p, k] for k in range(N_KEYS)]
            m = _tree(jnp.maximum, w)
            idx = _first_index(w, m, 8)
            vals[p, a] = m
            if p == 0:
                idx1[a] = idx
            for k in range(N_KEYS):
                sel = idx == float(k)
                work[p, k] = jnp.where(sel, -jnp.inf, w[k])
                if p == 1:
                    rank2[k] = jnp.where(sel, af, rank2[k])
        return carry

    lax.fori_loop(0, K, extract, 0)

    v1 = [vals[0, a] for a in range(K)]
    v2 = [vals[1, b] for b in range(K)]
    plen = [jnp.zeros((SUB, LANE), F32) for _ in range(K)]
    front = [v1[a] + v2[0] for a in range(K)]
    for _ in range(K):
        m = _tree(jnp.maximum, front)
        a_star = _first_index(front, m, 2)
        lsel = jnp.zeros((SUB, LANE), F32)
        v1sel = jnp.zeros((SUB, LANE), F32)
        sels = []
        for a in range(K):
            sel = a_star == float(a)
            sels.append(sel)
            plen[a] = plen[a] + jnp.where(sel, 1.0, 0.0)
            lsel = jnp.where(sel, plen[a], lsel)
            v1sel = jnp.where(sel, v1[a], v1sel)
        v2next = jnp.zeros((SUB, LANE), F32)
        for b in range(1, K):
            v2next = jnp.where(lsel == float(b), v2[b], v2next)
        fnew = jnp.where(lsel >= float(K), -jnp.inf, v1sel + v2next)
        for a in range(K):
            front[a] = jnp.where(sels[a], fnew, front[a])
    e1s = [jnp.exp(v1[a] - v1[0]) for a in range(K)]
    z = jnp.zeros((SUB, LANE), F32)
    for b in range(K):
        inner = jnp.zeros((SUB, LANE), F32)
        for a in range(K):
            inner = inner + jnp.where(plen[a] > float(b), e1s[a], 0.0)
        z = z + jnp.exp(v2[b] - v2[0]) * inner
    rz = 1.0 / z
    ids = [idx1[a] for a in range(K)]
    for k in range(N_KEYS):
        g, r = rows(k)
        lc = jnp.zeros((SUB, LANE), F32)
        for a in range(K):
            lc = jnp.where(ids[a] == float(k), plen[a], lc)
        lc_ref[0, g, r, :] = lc
        e1_ref[0, g, r, :] = jnp.exp(score(0, k) - v1[0]) * rz
        r2_ref[0, g, r, :] = rank2[k]
        e2_ref[0, g, r, :] = jnp.exp(score(1, k) - v2[0])


def _peer_topk(st, *, T, tokens):
    groups = N_KEYS // SUB
    rows_per_step = tokens // LANE * SUB
    tab = pl.BlockSpec((1, groups, rows_per_step, LANE), lambda t, h: (h, 0, t, 0))
    shp = jax.ShapeDtypeStruct((PEER_HEADS, groups, T // LANE * SUB, LANE), F32)
    vreg = (SUB, LANE)
    return pl.pallas_call(
        _peer_topk_kernel,
        grid=(T // tokens, PEER_HEADS),
        in_specs=[pl.BlockSpec((2 * groups, rows_per_step, LANE), lambda t, h: (h, t, 0))],
        out_specs=[tab, tab, tab, tab],
        out_shape=[shp, shp, shp, shp],
        scratch_shapes=[pltpu.VMEM((2, N_KEYS) + vreg, F32), pltpu.VMEM((N_KEYS,) + vreg, F32),
                        pltpu.VMEM((2, PEER_TOPK) + vreg, F32), pltpu.VMEM((PEER_TOPK,) + vreg, F32)],
        compiler_params=_cparams(("parallel", "parallel")),
        name="peer_topk",
    )(st)


def _gelu(a):
    return 0.5 * a * (1.0 + lax.erf(a * (1.0 / math.sqrt(2.0))))


def _interleave(major, minor):
    out, done = [], 0
    for n, item in enumerate(major):
        out.append(item)
        want = (n + 1) * len(minor) // len(major)
        out += minor[done:want]
        done = want
    return out


def _peer_dense_kernel(xnt_ref, u_ref, vt_ref, r2_ref, lc_ref, e1_ref, e2_ref, out_ref, r2b, e2b, w_scr, act_scr,
                       *, tiles, nchunks):
    j = pl.program_id(1)
    D, tb = xnt_ref.shape
    nl = tb // LANE
    packed = (N_KEYS // BF16_ROWS, BF16_ROWS, LANE)

    @pl.when(j == 0)
    def _():
        out_ref[...] = jnp.zeros(out_ref.shape, F32)
        w_scr[...] = jnp.zeros(w_scr.shape, BF16)
        act_scr[...] = jnp.zeros(act_scr.shape, F32)
        for h in range(PEER_HEADS):
            for c in range(nl):
                r2b[h, c] = r2_ref[h, :, c * SUB:(c + 1) * SUB, :].reshape(packed).astype(BF16)
                e2b[h, c] = e2_ref[h, :, c * SUB:(c + 1) * SUB, :].reshape(packed).astype(BF16)

    cur = j % 2
    prev = 1 - cur
    chunk = jnp.clip(j - 1, 0, nchunks - 1)
    zero = jnp.zeros((), BF16)
    halves = 2
    hw = tb // halves
    quarters = 4
    dp = D // quarters

    def scores(n):
        cs = slice(n * hw, (n + 1) * hw)
        act_scr[cur, :, cs] = _dot(u_ref[...], xnt_ref[:, cs])

    def combine(n):
        rs = slice(n * dp, (n + 1) * dp)
        out_ref[rs, :] += _dot(vt_ref[rs, :], w_scr[cur])

    def weights(ii, c):
        i = chunk * tiles + ii
        grp = lax.shift_right_logical(i, 3)
        sub = jnp.bitwise_and(i, SUB - 1)
        g = jnp.zeros(packed, BF16)
        for h in range(PEER_HEADS):
            lc = lc_ref[h, pl.ds(grp, 1), pl.ds(c * SUB + sub, 1), :].reshape(1, LANE)
            e1 = e1_ref[h, pl.ds(grp, 1), pl.ds(c * SUB + sub, 1), :].reshape(1, LANE)
            lc = jnp.broadcast_to(lc, (BF16_ROWS, LANE)).astype(BF16)
            e1 = jnp.broadcast_to(e1, (BF16_ROWS, LANE)).astype(BF16)
            g = g + jnp.where(r2b[h, c] < lc[None], e2b[h, c], zero) * e1[None]
        rs = slice(ii * N_KEYS, (ii + 1) * N_KEYS)
        cs = slice(c * LANE, (c + 1) * LANE)
        a = _gelu(act_scr[prev, rs, cs]).astype(BF16)
        w_scr[prev, rs, cs] = g.reshape(N_KEYS, LANE) * a

    mxu = [functools.partial(combine, 0), functools.partial(scores, 0), functools.partial(combine, 1),
           functools.partial(combine, 2), functools.partial(scores, 1), functools.partial(combine, 3)]
    vpu = [functools.partial(weights, ii, c) for ii in range(tiles) for c in range(nl)]
    for stage in _interleave(mxu, vpu):
        stage()


def _peer_dense(xnt, u, vt, r2, lc, e1, e2, *, tb, ec):
    D, T = xnt.shape
    E = u.shape[0]
    nchunks = E // ec
    groups = N_KEYS // SUB
    nl = tb // LANE
    tab = pl.BlockSpec((PEER_HEADS, groups, nl * SUB, LANE), lambda t, j: (0, 0, t, 0))
    packed = (PEER_HEADS, nl, N_KEYS // BF16_ROWS, BF16_ROWS, LANE)
    return pl.pallas_call(
        functools.partial(_peer_dense_kernel, tiles=ec // N_KEYS, nchunks=nchunks),
        grid=(T // tb, nchunks + 2),
        in_specs=[pl.BlockSpec((D, tb), lambda t, j: (0, t)),
                  pl.BlockSpec((ec, D), lambda t, j: (jnp.minimum(j, nchunks - 1), 0)),
                  pl.BlockSpec((D, ec), lambda t, j: (0, jnp.clip(j - 2, 0, nchunks - 1))),
                  tab, tab, tab, tab],
        out_specs=pl.BlockSpec((D, tb), lambda t, j: (0, t)),
        out_shape=jax.ShapeDtypeStruct((D, T), F32),
        scratch_shapes=[pltpu.VMEM(packed, BF16), pltpu.VMEM(packed, BF16), pltpu.VMEM((2, ec, tb), BF16),
                        pltpu.VMEM((2, ec, tb), F32)],
        compiler_params=_cparams(("parallel", "arbitrary")),
        name="peer_dense",
    )(xnt, u, vt, r2, lc, e1, e2)


def _final_kernel(pt_ref, h1_ref, g_ref, out_ref, *, normalize):
    h2 = h1_ref[...] + pt_ref[...].T
    out_ref[...] = _rms(h2, g_ref[...]) if normalize else h2


def _final(pt, h1, g, *, tb, normalize):
    T, D = h1.shape
    return pl.pallas_call(
        functools.partial(_final_kernel, normalize=normalize),
        grid=(T // tb,),
        in_specs=[pl.BlockSpec((D, tb), lambda i: (0, i)), pl.BlockSpec((tb, D), lambda i: (i, 0)),
                  pl.BlockSpec(g.shape, lambda i: (0, 0))],
        out_specs=pl.BlockSpec((tb, D), lambda i: (i, 0)),
        out_shape=jax.ShapeDtypeStruct((T, D), F32),
        compiler_params=_cparams(("parallel",)),
        name="final_norm",
    )(pt, h1, g)


def _rot_cols(w):
    half = w.shape[-1] // 2
    return jnp.concatenate([-w[..., half:], w[..., :half]], axis=-1)


def _layer(h2d, pos2, invf, p, *, B, S):
    T, D = h2d.shape
    w_in = p["w_in"]
    o0, o1, o2, o3 = D, 2 * D, 2 * D + Q_LORA, 2 * D + Q_LORA + KV_LORA + QK_ROPE
    w_kr = w_in[:, o2 + KV_LORA:o3]
    zpad = jnp.zeros((D, LANE - QK_ROPE), F32)
    wkv = jnp.concatenate([w_in[:, o2:o2 + KV_LORA], w_kr, zpad, _rot_cols(w_kr), zpad], axis=1)
    hglu, cq, ckv, gates = _in_proj(
        h2d, p["mix_norm_g"][None, :], w_in[:, :o0].astype(BF16), w_in[:, o0:o1].astype(BF16),
        w_in[:, o1:o2].astype(BF16), wkv.astype(BF16), w_in[:, o3:].astype(BF16), p["b_gate"][None, :], tm=512)

    w_uq = p["w_uq"].reshape(Q_LORA, N_HEADS, QK_NOPE + QK_ROPE)
    zq = jnp.zeros((Q_LORA, N_HEADS, LANE - QK_ROPE), F32)
    wqa = jnp.concatenate([w_uq, zq], axis=-1).reshape(Q_LORA, N_HEADS * QK_PAD)
    wqr = jnp.concatenate([_rot_cols(w_uq[..., QK_NOPE:]), zq], axis=-1).reshape(Q_LORA, N_HEADS * LANE)
    w_ukv = p["w_ukv"].reshape(KV_LORA, N_HEADS, QK_NOPE + V_DIM)
    wk = w_ukv[..., :QK_NOPE].reshape(KV_LORA, N_HEADS * QK_NOPE)
    wv = w_ukv[..., QK_NOPE:].reshape(KV_LORA, N_HEADS * V_DIM)
    q, k, v = _mla_proj(cq, ckv, pos2, invf, p["q_norm_g"][None, :], wqa.astype(BF16), wqr.astype(BF16),
                        p["kv_norm_g"][None, :], wk.astype(BF16), wv.T.astype(BF16), B=B, S=S, tm=512)
    o = _flash_attn(q, k, v, bq=512, heads=2)

    h1 = _mix(hglu, o.reshape(T, D), gates, h2d, p["w_dw"].reshape(CONV_WIDTH, D), p["b_dw"][None, :],
              p["conv_ln_g"][None, :], p["conv_ln_b"][None, :], p["w_conv_out"].astype(BF16),
              p["w_attn_out"].astype(BF16), p["w_o"].astype(BF16), S=S, tm=256)

    sk = p["peer_sub_keys"].reshape(2 * PEER_HEADS, N_KEYS, -1).astype(BF16)
    xnt, st = _peer_prep(h1, p["ffn_norm_g"][None, :], p["w_peer_q"].T.astype(BF16), sk, tm=512)
    r2, lc, e1, e2 = _peer_topk(st, T=T, tokens=1024)
    pt = _peer_dense(xnt, p["peer_u"].astype(BF16), p["peer_v"].T.astype(BF16), r2, lc, e1, e2, tb=512, ec=512)
    return h1, pt


def kernel(x, positions, mix_norm_g, w_in, b_gate, w_dw, b_dw, conv_ln_g, conv_ln_b, w_conv_out, q_norm_g, w_uq,
           kv_norm_g, w_ukv, w_attn_out, w_o, ffn_norm_g, w_peer_q, peer_sub_keys, peer_u, peer_v, final_norm_g):
    B, S, D = x.shape
    T = B * S
    stacked = dict(mix_norm_g=mix_norm_g, w_in=w_in, b_gate=b_gate, w_dw=w_dw, b_dw=b_dw, conv_ln_g=conv_ln_g,
                   conv_ln_b=conv_ln_b, w_conv_out=w_conv_out, q_norm_g=q_norm_g, w_uq=w_uq, kv_norm_g=kv_norm_g,
                   w_ukv=w_ukv, w_attn_out=w_attn_out, w_o=w_o, ffn_norm_g=ffn_norm_g, w_peer_q=w_peer_q,
                   peer_sub_keys=peer_sub_keys, peer_u=peer_u, peer_v=peer_v)
    depth = w_in.shape[0]
    half = QK_ROPE // 2
    inv_freq = ROPE_BASE ** (-jnp.arange(half, dtype=F32) / half)
    invf = jnp.concatenate([inv_freq, inv_freq, jnp.zeros((LANE - QK_ROPE,), F32)])[None, :]
    pos2 = positions.reshape(T, 1)
    h = x.reshape(T, D)
    for l in range(depth):
        p = {name: a[l] for name, a in stacked.items()}
        h1, pt = _layer(h, pos2, invf, p, B=B, S=S)
        last = l + 1 == depth
        h = _final(pt, h1, final_norm_g[None, :], tb=512, normalize=last)
    return h.reshape(B, S, D)
```

```python
import functools
import math

import jax
import jax.numpy as jnp
from jax import lax
from jax.experimental import pallas as pl
from jax.experimental.pallas import tpu as pltpu

F32 = jnp.float32
BF16 = jnp.bfloat16

EPS = 1e-6
CONV_WIDTH = 31
N_HEADS = 8
QK_NOPE = 128
QK_ROPE = 64
V_DIM = 128
Q_LORA = 256
KV_LORA = 128
ROPE_BASE = 10000.0
N_KEYS = 128
PEER_HEADS = 8
PEER_TOPK = 16
QK_PAD = 256
LANE = 128
SUB = 8
BF16_ROWS = 16
HALO = 32
VMEM_LIMIT = 56 * 1024 * 1024


def _cparams(sem):
    return pltpu.CompilerParams(dimension_semantics=sem, vmem_limit_bytes=VMEM_LIMIT)


def _dot(a, b):
    return jnp.dot(a, b, preferred_element_type=F32)


def _dot_nt(a, b):
    return lax.dot_general(a, b, (((1,), (1,)), ((), ())), preferred_element_type=F32)


def _sigmoid(x):
    return 1.0 / (1.0 + jnp.exp(-x))


def _rms(x, g):
    return x * lax.rsqrt(jnp.mean(x * x, axis=-1, keepdims=True) + EPS) * g


def _inproj_kernel(x_ref, g_ref, wv_ref, wg_ref, wq_ref, wkv_ref, wgl_ref, bg_ref,
                   hglu_ref, cq_ref, ckv_ref, gates_ref):
    a = _rms(x_ref[...], g_ref[...]).astype(BF16)
    val = _dot(a, wv_ref[...])
    gate = _dot(a, wg_ref[...])
    hglu_ref[...] = val * _sigmoid(gate)
    cq_ref[...] = _dot(a, wq_ref[...])
    ckv_ref[...] = _dot(a, wkv_ref[...])
    gates_ref[...] = _sigmoid(_dot(a, wgl_ref[...]) + bg_ref[...]).astype(BF16)


def _in_proj(x2, g, wv, wg, wq, wkv, wgl, bg, *, tm):
    T, D = x2.shape
    full = lambda a: pl.BlockSpec(a.shape, lambda i: (0, 0))
    row = lambda n: pl.BlockSpec((tm, n), lambda i: (i, 0))
    return pl.pallas_call(
        _inproj_kernel,
        grid=(T // tm,),
        in_specs=[row(D), full(g), full(wv), full(wg), full(wq), full(wkv), full(wgl), full(bg)],
        out_specs=[row(wv.shape[1]), row(wq.shape[1]), row(wkv.shape[1]), row(wgl.shape[1])],
        out_shape=[jax.ShapeDtypeStruct((T, wv.shape[1]), F32),
                   jax.ShapeDtypeStruct((T, wq.shape[1]), F32),
                   jax.ShapeDtypeStruct((T, wkv.shape[1]), F32),
                   jax.ShapeDtypeStruct((T, wgl.shape[1]), BF16)],
        compiler_params=_cparams(("parallel",)),
        name="in_proj",
    )(x2, g, wv, wg, wq, wkv, wgl, bg)


def _mla_proj_kernel(cq_ref, ckv_ref, pos_ref, invf_ref, qg_ref, wqa_ref, wqr_ref, kvg_ref, wk_ref, wv_ref,
                     q_ref, k_ref, v_ref, *, scale):
    qn = _rms(cq_ref[...], qg_ref[...]).astype(BF16)
    ckv_full = ckv_ref[...]
    kvn = _rms(ckv_full[:, :KV_LORA], kvg_ref[...]).astype(BF16)
    ang = pos_ref[...].astype(F32) * invf_ref[...]
    cosv = jnp.cos(ang)
    sinv = jnp.sin(ang)
    kr = (ckv_full[:, KV_LORA:KV_LORA + LANE] * cosv + ckv_full[:, KV_LORA + LANE:] * sinv).astype(BF16)
    qa = _dot(qn, wqa_ref[...])
    qr = _dot(qn, wqr_ref[...])
    kn = _dot(kvn, wk_ref[...])
    vt = _dot_nt(wv_ref[...], kvn)
    for h in range(N_HEADS):
        q_ref[0, h, :, 0:LANE] = (qa[:, h * QK_PAD:h * QK_PAD + LANE] * scale).astype(BF16)
        q_rope = qa[:, h * QK_PAD + LANE:(h + 1) * QK_PAD] * cosv + qr[:, h * LANE:(h + 1) * LANE] * sinv
        q_ref[0, h, :, LANE:QK_PAD] = (q_rope * scale).astype(BF16)
        k_ref[0, h, :, 0:LANE] = kn[:, h * LANE:(h + 1) * LANE].astype(BF16)
        k_ref[0, h, :, LANE:QK_PAD] = kr
        v_ref[0, h, :, :] = vt[h * V_DIM:(h + 1) * V_DIM, :].astype(BF16)


def _mla_proj(cq, ckv, pos2, invf, qg, wqa, wqr, kvg, wk, wv, *, B, S, tm):
    T = cq.shape[0]
    nb = S // tm
    full = lambda a: pl.BlockSpec(a.shape, lambda i: (0, 0))
    row = lambda n: pl.BlockSpec((tm, n), lambda i: (i, 0))
    head = lambda n: pl.BlockSpec((1, N_HEADS, tm, n), lambda i: (i // nb, 0, i % nb, 0))
    head_t = pl.BlockSpec((1, N_HEADS, V_DIM, tm), lambda i: (i // nb, 0, 0, i % nb))
    scale = math.log2(math.e) / math.sqrt(QK_NOPE + QK_ROPE)
    return pl.pallas_call(
        functools.partial(_mla_proj_kernel, scale=scale),
        grid=(T // tm,),
        in_specs=[row(cq.shape[1]), row(ckv.shape[1]), row(1), full(invf), full(qg), full(wqa), full(wqr),
                  full(kvg), full(wk), full(wv)],
        out_specs=[head(QK_PAD), head(QK_PAD), head_t],
        out_shape=[jax.ShapeDtypeStruct((B, N_HEADS, S, QK_PAD), BF16),
                   jax.ShapeDtypeStruct((B, N_HEADS, S, QK_PAD), BF16),
                   jax.ShapeDtypeStruct((B, N_HEADS, V_DIM, S), BF16)],
        compiler_params=_cparams(("parallel",)),
        name="mla_proj",
    )(cq, ckv, pos2, invf, qg, wqa, wqr, kvg, wk, wv)


def _flash_kernel(q_ref, k_ref, vt_ref, o_ref, acc_scr, *, bq, bk, heads):
    qi = pl.program_id(2)
    acc_scr[...] = jnp.zeros(acc_scr.shape, F32)

    def step(ki, carry, diagonal):
        off = pl.multiple_of(ki * bk, bk)
        out = []
        for hh in range(heads):
            m_prev, l_prev = carry[2 * hh], carry[2 * hh + 1]
            st = _dot_nt(k_ref[0, hh, pl.ds(off, bk), :], q_ref[0, hh])
            if diagonal:
                kpos = lax.broadcasted_iota(jnp.int32, (bk, bq), 0)
                qpos = lax.broadcasted_iota(jnp.int32, (bk, bq), 1)
                st = jnp.where(kpos <= qpos, st, -1e30)
            m_new = jnp.maximum(m_prev, jnp.max(st, axis=0, keepdims=True))
            p = jnp.exp2(st - m_new)
            alpha = jnp.exp2(m_prev - m_new)
            l_new = alpha * l_prev + jnp.sum(p, axis=0, keepdims=True)
            pv = _dot(vt_ref[0, hh, :, pl.ds(off, bk)], p.astype(BF16))
            acc_scr[hh] = alpha * acc_scr[hh] + pv
            out += [m_new, l_new]
        return tuple(out)

    init = (jnp.full((1, bq), -1e30, F32), jnp.zeros((1, bq), F32)) * heads
    carry = lax.fori_loop(0, qi, lambda ki, c: step(ki, c, False), init)
    carry = step(qi, carry, True)
    for hh in range(heads):
        o_ref[0, :, hh * V_DIM:(hh + 1) * V_DIM] = (acc_scr[hh] / carry[2 * hh + 1]).T.astype(BF16)


def _flash_attn(q, k, vt, *, bq, heads):
    B, H, S, _ = q.shape
    return pl.pallas_call(
        functools.partial(_flash_kernel, bq=bq, bk=bq, heads=heads),
        grid=(B, H // heads, S // bq),
        in_specs=[pl.BlockSpec((1, heads, bq, QK_PAD), lambda b, h, i: (b, h, i, 0)),
                  pl.BlockSpec((1, heads, S, QK_PAD), lambda b, h, i: (b, h, 0, 0)),
                  pl.BlockSpec((1, heads, V_DIM, S), lambda b, h, i: (b, h, 0, 0))],
        out_specs=pl.BlockSpec((1, bq, heads * V_DIM), lambda b, h, i: (b, i, h)),
        out_shape=jax.ShapeDtypeStruct((B, S, H * V_DIM), BF16),
        scratch_shapes=[pltpu.VMEM((heads, V_DIM, bq), F32)],
        compiler_params=_cparams(("parallel", "parallel", "arbitrary")),
        name="flash_attn",
    )(q, k, vt)


def _mix_kernel(hc_ref, hp_ref, o_ref, gates_ref, x_ref, wdw_ref, bdw_ref, lng_ref, lnb_ref,
                wco_ref, wao_ref, wo_ref, h1_ref, hbuf, cbuf, *, tm, blocks_per_seq, rows):
    i = pl.program_id(0)
    first = (i % blocks_per_seq) == 0
    hbuf[0, 0:HALO, :] = jnp.where(first, 0.0, hp_ref[...])
    hbuf[0, HALO:HALO + tm, :] = hc_ref[...]
    D = hc_ref.shape[1]
    span = HALO + tm - SUB
    for s in range(1, SUB):
        hbuf[s, 0:span, :] = hbuf[0, s:s + span, :]
    base = HALO - (CONV_WIDTH - 1)
    for c in range(D // LANE):
        cs = slice(c * LANE, (c + 1) * LANE)
        for r in range(tm // rows):
            acc = jnp.broadcast_to(bdw_ref[:, cs], (rows, LANE))
            for kk in range(CONV_WIDTH):
                shift = (base + kk) % SUB
                start = r * rows + base + kk - shift
                acc = acc + wdw_ref[kk:kk + 1, cs] * hbuf[shift, start:start + rows, cs]
            cbuf[r * rows:(r + 1) * rows, cs] = acc
    y = cbuf[...]
    mu = jnp.mean(y, axis=-1, keepdims=True)
    yc = y - mu
    var = jnp.mean(yc * yc, axis=-1, keepdims=True)
    yn = yc * lax.rsqrt(var + EPS) * lng_ref[...] + lnb_ref[...]
    act = (yn * _sigmoid(yn)).astype(BF16)
    y_conv = _dot(act, wco_ref[...])
    y_attn = _dot(o_ref[...], wao_ref[...])
    gts = gates_ref[...].astype(F32)
    mixed = (gts[:, :D] * y_conv + gts[:, D:] * y_attn).astype(BF16)
    h1_ref[...] = x_ref[...] + _dot(mixed, wo_ref[...])


def _mix(hglu, o2, gates, x2, wdw, bdw, lng, lnb, wco, wao, wo, *, S, tm):
    T, D = x2.shape
    full = lambda a: pl.BlockSpec(a.shape, lambda i: (0, 0))
    row = lambda n: pl.BlockSpec((tm, n), lambda i: (i, 0))
    per = tm // HALO
    halo = pl.BlockSpec((HALO, D), lambda i: (jnp.maximum(i * per - 1, 0), 0))
    return pl.pallas_call(
        functools.partial(_mix_kernel, tm=tm, blocks_per_seq=S // tm, rows=64),
        grid=(T // tm,),
        in_specs=[row(D), halo, row(D), row(2 * D), row(D), full(wdw), full(bdw), full(lng), full(lnb),
                  full(wco), full(wao), full(wo)],
        out_specs=row(D),
        out_shape=jax.ShapeDtypeStruct((T, D), F32),
        scratch_shapes=[pltpu.VMEM((SUB, HALO + tm, D), F32), pltpu.VMEM((tm, D), F32)],
        compiler_params=_cparams(("parallel",)),
        name="mix",
    )(hglu, hglu, o2, gates, x2, wdw, bdw, lng, lnb, wco, wao, wo)


def _peer_prep_kernel(h1_ref, g_ref, wpqt_ref, sk_ref, xnt_ref, st_ref):
    tm = h1_ref.shape[0]
    xn32 = _rms(h1_ref[...], g_ref[...])
    xnt_ref[...] = xn32.T.astype(BF16)
    xn = xn32.astype(BF16)
    qt = _dot_nt(wpqt_ref[...], xn).astype(BF16)
    groups = N_KEYS // SUB
    for hp in range(2 * PEER_HEADS):
        rs = slice(hp * N_KEYS, (hp + 1) * N_KEYS)
        s = _dot(sk_ref[hp], qt[rs, :])
        for a in range(tm // LANE):
            st_ref[hp * groups:(hp + 1) * groups, a * SUB:(a + 1) * SUB, :] = (
                s[:, a * LANE:(a + 1) * LANE].reshape(groups, SUB, LANE))


def _peer_prep(h1, g, wpqt, sk, *, tm):
    T, D = h1.shape
    G = wpqt.shape[0] // SUB
    return pl.pallas_call(
        _peer_prep_kernel,
        grid=(T // tm,),
        in_specs=[pl.BlockSpec((tm, D), lambda i: (i, 0)), pl.BlockSpec(g.shape, lambda i: (0, 0)),
                  pl.BlockSpec(wpqt.shape, lambda i: (0, 0)), pl.BlockSpec(sk.shape, lambda i: (0, 0, 0))],
        out_specs=[pl.BlockSpec((D, tm), lambda i: (0, i)),
                   pl.BlockSpec((G, tm // LANE * SUB, LANE), lambda i: (0, i, 0))],
        out_shape=[jax.ShapeDtypeStruct((D, T), BF16), jax.ShapeDtypeStruct((G, T // LANE * SUB, LANE), F32)],
        compiler_params=_cparams(("parallel",)),
        name="peer_prep",
    )(h1, g, wpqt, sk)


def _tree(op, xs):
    xs = list(xs)
    while len(xs) > 1:
        xs = [op(xs[i], xs[i + 1]) for i in range(0, len(xs) - 1, 2)] + ([xs[-1]] if len(xs) % 2 else [])
    return xs[0]


def _first_index(xs, m, chains):
    n = len(xs)
    per = n // chains
    heads = []
    for c in range(chains):
        idx = jnp.full(m.shape, float(n), F32)
        for i in reversed(range(c * per, (c + 1) * per)):
            idx = jnp.where(xs[i] == m, float(i), idx)
        heads.append(idx)
    return _tree(jnp.minimum, heads)


def _peer_topk_kernel(st_ref, r2_ref, lc_ref, e1_ref, e2_ref, work, rank2, vals, idx1):
    K = PEER_TOPK

    def rows(k):
        return k // SUB, pl.ds(k % SUB, SUB, stride=SUB)

    def score(p, k):
        g, r = rows(p * N_KEYS + k)
        return st_ref[g, r, :]

    for p in range(2):
        for k in range(N_KEYS):
            work[p, k] = score(p, k)
    for k in range(N_KEYS):
        rank2[k] = jnp.full((SUB, LANE), float(K), F32)

    def extract(a, carry):
        af = lax.convert_element_type(a, F32)
        for p in range(2):
            w = [work[p, k] for k in range(N_KEYS)]
            m = _tree(jnp.maximum, w)
            idx = _first_index(w, m, 8)
            vals[p, a] = m
            if p == 0:
                idx1[a] = idx
            for k in range(N_KEYS):
                sel = idx == float(k)
                work[p, k] = jnp.where(sel, -jnp.inf, w[k])
                if p == 1:
                    rank2[k] = jnp.where(sel, af, rank2[k])
        return carry

    lax.fori_loop(0, K, extract, 0)

    v1 = [vals[0, a] for a in range(K)]
    v2 = [vals[1, b] for b in range(K)]
    plen = [jnp.zeros((SUB, LANE), F32) for _ in range(K)]
    front = [v1[a] + v2[0] for a in range(K)]
    for _ in range(K):
        m = _tree(jnp.maximum, front)
        a_star = _first_index(front, m, 2)
        lsel = jnp.zeros((SUB, LANE), F32)
        v1sel = jnp.zeros((SUB, LANE), F32)
        sels = []
        for a in range(K):
            sel = a_star == float(a)
            sels.append(sel)
            plen[a] = plen[a] + jnp.where(sel, 1.0, 0.0)
            lsel = jnp.where(sel, plen[a], lsel)
            v1sel = jnp.where(sel, v1[a], v1sel)
        v2next = jnp.zeros((SUB, LANE), F32)
        for b in range(1, K):
            v2next = jnp.where(lsel == float(b), v2[b], v2next)
        fnew = jnp.where(lsel >= float(K), -jnp.inf, v1sel + v2next)
        for a in range(K):
            front[a] = jnp.where(sels[a], fnew, front[a])
    e1s = [jnp.exp(v1[a] - v1[0]) for a in range(K)]
    z = jnp.zeros((SUB, LANE), F32)
    for b in range(K):
        inner = jnp.zeros((SUB, LANE), F32)
        for a in range(K):
            inner = inner + jnp.where(plen[a] > float(b), e1s[a], 0.0)
        z = z + jnp.exp(v2[b] - v2[0]) * inner
    rz = 1.0 / z
    ids = [idx1[a] for a in range(K)]
    for k in range(N_KEYS):
        g, r = rows(k)
        lc = jnp.zeros((SUB, LANE), F32)
        for a in range(K):
            lc = jnp.where(ids[a] == float(k), plen[a], lc)
        lc_ref[0, g, r, :] = lc
        e1_ref[0, g, r, :] = jnp.exp(score(0, k) - v1[0]) * rz
        r2_ref[0, g, r, :] = rank2[k]
        e2_ref[0, g, r, :] = jnp.exp(score(1, k) - v2[0])


def _peer_topk(st, *, T, tokens):
    groups = N_KEYS // SUB
    rows_per_step = tokens // LANE * SUB
    tab = pl.BlockSpec((1, groups, rows_per_step, LANE), lambda t, h: (h, 0, t, 0))
    shp = jax.ShapeDtypeStruct((PEER_HEADS, groups, T // LANE * SUB, LANE), F32)
    vreg = (SUB, LANE)
    return pl.pallas_call(
        _peer_topk_kernel,
        grid=(T // tokens, PEER_HEADS),
        in_specs=[pl.BlockSpec((2 * groups, rows_per_step, LANE), lambda t, h: (h, t, 0))],
        out_specs=[tab, tab, tab, tab],
        out_shape=[shp, shp, shp, shp],
        scratch_shapes=[pltpu.VMEM((2, N_KEYS) + vreg, F32), pltpu.VMEM((N_KEYS,) + vreg, F32),
                        pltpu.VMEM((2, PEER_TOPK) + vreg, F32), pltpu.VMEM((PEER_TOPK,) + vreg, F32)],
        compiler_params=_cparams(("parallel", "parallel")),
        name="peer_topk",
    )(st)


def _gelu(a):
    return 0.5 * a * (1.0 + lax.erf(a * (1.0 / math.sqrt(2.0))))


def _interleave(major, minor):
    out, done = [], 0
    for n, item in enumerate(major):
        out.append(item)
        want = (n + 1) * len(minor) // len(major)
        out += minor[done:want]
        done = want
    return out


def _peer_dense_kernel(xnt_ref, u_ref, vt_ref, r2_ref, lc_ref, e1_ref, e2_ref, out_ref, r2b, e2b, w_scr, act_scr,
                       *, nchunks, nwork):
    s = pl.program_id(0)
    D, tb = xnt_ref.shape
    ec = u_ref.shape[0]
    tiles = ec // N_KEYS
    nl = tb // LANE
    packed = (N_KEYS // BF16_ROWS, BF16_ROWS, LANE)
    w_item = jnp.clip(s - 1, 0, nwork - 1)
    c_item = jnp.clip(s - 2, 0, nwork - 1)
    chunk = w_item % nchunks

    @pl.when(s == 0)
    def _():
        w_scr[...] = jnp.zeros(w_scr.shape, BF16)
        act_scr[...] = jnp.zeros(act_scr.shape, F32)

    @pl.when(chunk == 0)
    def _():
        for h in range(PEER_HEADS):
            for c in range(nl):
                r2b[h, c] = r2_ref[h, :, c * SUB:(c + 1) * SUB, :].reshape(packed).astype(BF16)
                e2b[h, c] = e2_ref[h, :, c * SUB:(c + 1) * SUB, :].reshape(packed).astype(BF16)

    @pl.when(c_item % nchunks == 0)
    def _():
        out_ref[...] = jnp.zeros(out_ref.shape, F32)

    cur = s % 2
    prev = 1 - cur
    zero = jnp.zeros((), BF16)
    halves = 2
    hw = tb // halves
    quarters = 4
    dp = D // quarters
    share = 2

    def scores(n):
        cs = slice(n * hw, (n + 1) * hw)
        act_scr[cur, :, cs] = _dot(u_ref[...], xnt_ref[:, cs])

    def combine(n):
        rs = slice(n * dp, (n + 1) * dp)
        out_ref[rs, :] += _dot(vt_ref[rs, :], w_scr[cur])

    def weights(i0, c):
        gs = [jnp.zeros(packed, BF16) for _ in range(share)]
        for h in range(PEER_HEADS):
            r2 = r2b[h, c]
            e2 = e2b[h, c]
            for n in range(share):
                row = c * SUB + i0 + n
                lc = jnp.broadcast_to(lc_ref[h, pl.ds(chunk, 1), row, :], (BF16_ROWS, LANE)).astype(BF16)
                e1 = jnp.broadcast_to(e1_ref[h, pl.ds(chunk, 1), row, :], (BF16_ROWS, LANE)).astype(BF16)
                gs[n] = gs[n] + jnp.where(r2 < lc[None], e2, zero) * e1[None]
        cs = slice(c * LANE, (c + 1) * LANE)
        for n in range(share):
            rs = slice((i0 + n) * N_KEYS, (i0 + n + 1) * N_KEYS)
            a = _gelu(act_scr[prev, rs, cs]).astype(BF16)
            w_scr[prev, rs, cs] = gs[n].reshape(N_KEYS, LANE) * a

    mxu = [functools.partial(combine, 0), functools.partial(scores, 0), functools.partial(combine, 1),
           functools.partial(combine, 2), functools.partial(scores, 1), functools.partial(combine, 3)]
    vpu = [functools.partial(weights, i0, c) for i0 in range(0, tiles, share) for c in range(nl)]
    for stage in _interleave(mxu, vpu):
        stage()


def _peer_dense(xnt, u, vt, r2, lc, e1, e2, *, tb, ec):
    D, T = xnt.shape
    E = u.shape[0]
    assert ec == N_KEYS * SUB
    nchunks = E // ec
    nwork = (T // tb) * nchunks
    groups = N_KEYS // SUB
    nl = tb // LANE
    s_item = lambda s: jnp.minimum(s, nwork - 1)
    w_item = lambda s: jnp.clip(s - 1, 0, nwork - 1)
    c_item = lambda s: jnp.clip(s - 2, 0, nwork - 1)
    tab = pl.BlockSpec((PEER_HEADS, groups, nl * SUB, LANE), lambda s: (0, 0, w_item(s) // nchunks, 0))
    packed = (PEER_HEADS, nl, N_KEYS // BF16_ROWS, BF16_ROWS, LANE)
    return pl.pallas_call(
        functools.partial(_peer_dense_kernel, nchunks=nchunks, nwork=nwork),
        grid=(nwork + 2,),
        in_specs=[pl.BlockSpec((D, tb), lambda s: (0, s_item(s) // nchunks)),
                  pl.BlockSpec((ec, D), lambda s: (s_item(s) % nchunks, 0)),
                  pl.BlockSpec((D, ec), lambda s: (0, c_item(s) % nchunks)),
                  tab, tab, tab, tab],
        out_specs=pl.BlockSpec((D, tb), lambda s: (0, c_item(s) // nchunks)),
        out_shape=jax.ShapeDtypeStruct((D, T), F32),
        scratch_shapes=[pltpu.VMEM(packed, BF16), pltpu.VMEM(packed, BF16), pltpu.VMEM((2, ec, tb), BF16),
                        pltpu.VMEM((2, ec, tb), F32)],
        compiler_params=_cparams(("arbitrary",)),
        name="peer_dense",
    )(xnt, u, vt, r2, lc, e1, e2)


def _final_kernel(pt_ref, h1_ref, g_ref, out_ref, *, normalize):
    h2 = h1_ref[...] + pt_ref[...].T
    out_ref[...] = _rms(h2, g_ref[...]) if normalize else h2


def _final(pt, h1, g, *, tb, normalize):
    T, D = h1.shape
    return pl.pallas_call(
        functools.partial(_final_kernel, normalize=normalize),
        grid=(T // tb,),
        in_specs=[pl.BlockSpec((D, tb), lambda i: (0, i)), pl.BlockSpec((tb, D), lambda i: (i, 0)),
                  pl.BlockSpec(g.shape, lambda i: (0, 0))],
        out_specs=pl.BlockSpec((tb, D), lambda i: (i, 0)),
        out_shape=jax.ShapeDtypeStruct((T, D), F32),
        compiler_params=_cparams(("parallel",)),
        name="final_norm",
    )(pt, h1, g)


def _rot_cols(w):
    half = w.shape[-1] // 2
    return jnp.concatenate([-w[..., half:], w[..., :half]], axis=-1)


def _layer(h2d, pos2, invf, p, *, B, S):
    T, D = h2d.shape
    w_in = p["w_in"]
    o0, o1, o2, o3 = D, 2 * D, 2 * D + Q_LORA, 2 * D + Q_LORA + KV_LORA + QK_ROPE
    w_kr = w_in[:, o2 + KV_LORA:o3]
    zpad = jnp.zeros((D, LANE - QK_ROPE), F32)
    wkv = jnp.concatenate([w_in[:, o2:o2 + KV_LORA], w_kr, zpad, _rot_cols(w_kr), zpad], axis=1)
    hglu, cq, ckv, gates = _in_proj(
        h2d, p["mix_norm_g"][None, :], w_in[:, :o0].astype(BF16), w_in[:, o0:o1].astype(BF16),
        w_in[:, o1:o2].astype(BF16), wkv.astype(BF16), w_in[:, o3:].astype(BF16), p["b_gate"][None, :], tm=512)

    w_uq = p["w_uq"].reshape(Q_LORA, N_HEADS, QK_NOPE + QK_ROPE)
    zq = jnp.zeros((Q_LORA, N_HEADS, LANE - QK_ROPE), F32)
    wqa = jnp.concatenate([w_uq, zq], axis=-1).reshape(Q_LORA, N_HEADS * QK_PAD)
    wqr = jnp.concatenate([_rot_cols(w_uq[..., QK_NOPE:]), zq], axis=-1).reshape(Q_LORA, N_HEADS * LANE)
    w_ukv = p["w_ukv"].reshape(KV_LORA, N_HEADS, QK_NOPE + V_DIM)
    wk = w_ukv[..., :QK_NOPE].reshape(KV_LORA, N_HEADS * QK_NOPE)
    wv = w_ukv[..., QK_NOPE:].reshape(KV_LORA, N_HEADS * V_DIM)
    q, k, v = _mla_proj(cq, ckv, pos2, invf, p["q_norm_g"][None, :], wqa.astype(BF16), wqr.astype(BF16),
                        p["kv_norm_g"][None, :], wk.astype(BF16), wv.T.astype(BF16), B=B, S=S, tm=512)
    o = _flash_attn(q, k, v, bq=512, heads=2)

    h1 = _mix(hglu, o.reshape(T, D), gates, h2d, p["w_dw"].reshape(CONV_WIDTH, D), p["b_dw"][None, :],
              p["conv_ln_g"][None, :], p["conv_ln_b"][None, :], p["w_conv_out"].astype(BF16),
              p["w_attn_out"].astype(BF16), p["w_o"].astype(BF16), S=S, tm=256)

    sk = p["peer_sub_keys"].reshape(2 * PEER_HEADS, N_KEYS, -1).astype(BF16)
    xnt, st = _peer_prep(h1, p["ffn_norm_g"][None, :], p["w_peer_q"].T.astype(BF16), sk, tm=512)
    r2, lc, e1, e2 = _peer_topk(st, T=T, tokens=1024)
    pt = _peer_dense(xnt, p["peer_u"].astype(BF16), p["peer_v"].T.astype(BF16), r2, lc, e1, e2, tb=512, ec=1024)
    return h1, pt


def kernel(x, positions, mix_norm_g, w_in, b_gate, w_dw, b_dw, conv_ln_g, conv_ln_b, w_conv_out, q_norm_g, w_uq,
           kv_norm_g, w_ukv, w_attn_out, w_o, ffn_norm_g, w_peer_q, peer_sub_keys, peer_u, peer_v, final_norm_g):
    B, S, D = x.shape
    T = B * S
    stacked = dict(mix_norm_g=mix_norm_g, w_in=w_in, b_gate=b_gate, w_dw=w_dw, b_dw=b_dw, conv_ln_g=conv_ln_g,
                   conv_ln_b=conv_ln_b, w_conv_out=w_conv_out, q_norm_g=q_norm_g, w_uq=w_uq, kv_norm_g=kv_norm_g,
                   w_ukv=w_ukv, w_attn_out=w_attn_out, w_o=w_o, ffn_norm_g=ffn_norm_g, w_peer_q=w_peer_q,
                   peer_sub_keys=peer_sub_keys, peer_u=peer_u, peer_v=peer_v)
    depth = w_in.shape[0]
    half = QK_ROPE // 2
    inv_freq = ROPE_BASE ** (-jnp.arange(half, dtype=F32) / half)
    invf = jnp.concatenate([inv_freq, inv_freq, jnp.zeros((LANE - QK_ROPE,), F32)])[None, :]
    pos2 = positions.reshape(T, 1)
    h = x.reshape(T, D)
    for l in range(depth):
        p = {name: a[l] for name, a in stacked.items()}
        h1, pt = _layer(h, pos2, invf, p, B=B, S=S)
        last = l + 1 == depth
        h = _final(pt, h1, final_norm_g[None, :], tb=512, normalize=last)
    return h.reshape(B, S, D)
```

```python
import functools
import math

import jax
import jax.numpy as jnp
from jax import lax
from jax.experimental import pallas as pl
from jax.experimental.pallas import tpu as pltpu

F32 = jnp.float32
BF16 = jnp.bfloat16

EPS = 1e-6
CONV_WIDTH = 31
N_HEADS = 8
QK_NOPE = 128
QK_ROPE = 64
V_DIM = 128
Q_LORA = 256
KV_LORA = 128
ROPE_BASE = 10000.0
N_KEYS = 128
PEER_HEADS = 8
PEER_TOPK = 16
QK_PAD = 256
LANE = 128
SUB = 8
BF16_ROWS = 16
HALO = 32
VMEM_LIMIT = 56 * 1024 * 1024


def _cparams(sem):
    return pltpu.CompilerParams(dimension_semantics=sem, vmem_limit_bytes=VMEM_LIMIT)


def _dot(a, b):
    return jnp.dot(a, b, preferred_element_type=F32)


def _dot_nt(a, b):
    return lax.dot_general(a, b, (((1,), (1,)), ((), ())), preferred_element_type=F32)


def _sigmoid(x):
    return 1.0 / (1.0 + jnp.exp(-x))


def _rms(x, g):
    return x * lax.rsqrt(jnp.mean(x * x, axis=-1, keepdims=True) + EPS) * g


def _inproj_kernel(x_ref, g_ref, wv_ref, wg_ref, wq_ref, wkv_ref, wgl_ref, bg_ref,
                   hglu_ref, cq_ref, ckv_ref, gates_ref):
    a = _rms(x_ref[...], g_ref[...]).astype(BF16)
    val = _dot(a, wv_ref[...])
    gate = _dot(a, wg_ref[...])
    hglu_ref[...] = val * _sigmoid(gate)
    cq_ref[...] = _dot(a, wq_ref[...])
    ckv_ref[...] = _dot(a, wkv_ref[...])
    gates_ref[...] = _sigmoid(_dot(a, wgl_ref[...]) + bg_ref[...]).astype(BF16)


def _in_proj(x2, g, wv, wg, wq, wkv, wgl, bg, *, tm):
    T, D = x2.shape
    full = lambda a: pl.BlockSpec(a.shape, lambda i: (0, 0))
    row = lambda n: pl.BlockSpec((tm, n), lambda i: (i, 0))
    return pl.pallas_call(
        _inproj_kernel,
        grid=(T // tm,),
        in_specs=[row(D), full(g), full(wv), full(wg), full(wq), full(wkv), full(wgl), full(bg)],
        out_specs=[row(wv.shape[1]), row(wq.shape[1]), row(wkv.shape[1]), row(wgl.shape[1])],
        out_shape=[jax.ShapeDtypeStruct((T, wv.shape[1]), F32),
                   jax.ShapeDtypeStruct((T, wq.shape[1]), F32),
                   jax.ShapeDtypeStruct((T, wkv.shape[1]), F32),
                   jax.ShapeDtypeStruct((T, wgl.shape[1]), BF16)],
        compiler_params=_cparams(("parallel",)),
        name="in_proj",
    )(x2, g, wv, wg, wq, wkv, wgl, bg)


def _mla_proj_kernel(cq_ref, ckv_ref, pos_ref, invf_ref, qg_ref, wqa_ref, wqr_ref, kvg_ref, wk_ref, wv_ref,
                     q_ref, k_ref, v_ref, *, scale):
    qn = _rms(cq_ref[...], qg_ref[...]).astype(BF16)
    ckv_full = ckv_ref[...]
    kvn = _rms(ckv_full[:, :KV_LORA], kvg_ref[...]).astype(BF16)
    ang = pos_ref[...].astype(F32) * invf_ref[...]
    cosv = jnp.cos(ang)
    sinv = jnp.sin(ang)
    kr = (ckv_full[:, KV_LORA:KV_LORA + LANE] * cosv + ckv_full[:, KV_LORA + LANE:] * sinv).astype(BF16)
    qa = _dot(qn, wqa_ref[...])
    qr = _dot(qn, wqr_ref[...])
    kn = _dot(kvn, wk_ref[...])
    vt = _dot_nt(wv_ref[...], kvn)
    for h in range(N_HEADS):
        q_ref[0, h, :, 0:LANE] = (qa[:, h * QK_PAD:h * QK_PAD + LANE] * scale).astype(BF16)
        q_rope = qa[:, h * QK_PAD + LANE:(h + 1) * QK_PAD] * cosv + qr[:, h * LANE:(h + 1) * LANE] * sinv
        q_ref[0, h, :, LANE:QK_PAD] = (q_rope * scale).astype(BF16)
        k_ref[0, h, :, 0:LANE] = kn[:, h * LANE:(h + 1) * LANE].astype(BF16)
        k_ref[0, h, :, LANE:QK_PAD] = kr
        v_ref[0, h, :, :] = vt[h * V_DIM:(h + 1) * V_DIM, :].astype(BF16)


def _mla_proj(cq, ckv, pos2, invf, qg, wqa, wqr, kvg, wk, wv, *, B, S, tm):
    T = cq.shape[0]
    nb = S // tm
    full = lambda a: pl.BlockSpec(a.shape, lambda i: (0, 0))
    row = lambda n: pl.BlockSpec((tm, n), lambda i: (i, 0))
    head = lambda n: pl.BlockSpec((1, N_HEADS, tm, n), lambda i: (i // nb, 0, i % nb, 0))
    head_t = pl.BlockSpec((1, N_HEADS, V_DIM, tm), lambda i: (i // nb, 0, 0, i % nb))
    scale = math.log2(math.e) / math.sqrt(QK_NOPE + QK_ROPE)
    return pl.pallas_call(
        functools.partial(_mla_proj_kernel, scale=scale),
        grid=(T // tm,),
        in_specs=[row(cq.shape[1]), row(ckv.shape[1]), row(1), full(invf), full(qg), full(wqa), full(wqr),
                  full(kvg), full(wk), full(wv)],
        out_specs=[head(QK_PAD), head(QK_PAD), head_t],
        out_shape=[jax.ShapeDtypeStruct((B, N_HEADS, S, QK_PAD), BF16),
                   jax.ShapeDtypeStruct((B, N_HEADS, S, QK_PAD), BF16),
                   jax.ShapeDtypeStruct((B, N_HEADS, V_DIM, S), BF16)],
        compiler_params=_cparams(("parallel",)),
        name="mla_proj",
    )(cq, ckv, pos2, invf, qg, wqa, wqr, kvg, wk, wv)


def _flash_kernel(q_ref, k_ref, vt_ref, o_ref, acc_scr, *, bq, bk, heads):
    qi = pl.program_id(2)
    acc_scr[...] = jnp.zeros(acc_scr.shape, F32)

    def step(ki, carry, diagonal):
        off = pl.multiple_of(ki * bk, bk)
        out = []
        for hh in range(heads):
            m_prev, l_prev = carry[2 * hh], carry[2 * hh + 1]
            st = _dot_nt(k_ref[0, hh, pl.ds(off, bk), :], q_ref[0, hh])
            if diagonal:
                kpos = lax.broadcasted_iota(jnp.int32, (bk, bq), 0)
                qpos = lax.broadcasted_iota(jnp.int32, (bk, bq), 1)
                st = jnp.where(kpos <= qpos, st, -1e30)
            m_new = jnp.maximum(m_prev, jnp.max(st, axis=0, keepdims=True))
            p = jnp.exp2(st - m_new)
            alpha = jnp.exp2(m_prev - m_new)
            l_new = alpha * l_prev + jnp.sum(p, axis=0, keepdims=True)
            pv = _dot(vt_ref[0, hh, :, pl.ds(off, bk)], p.astype(BF16))
            acc_scr[hh] = alpha * acc_scr[hh] + pv
            out += [m_new, l_new]
        return tuple(out)

    init = (jnp.full((1, bq), -1e30, F32), jnp.zeros((1, bq), F32)) * heads
    carry = lax.fori_loop(0, qi, lambda ki, c: step(ki, c, False), init)
    carry = step(qi, carry, True)
    for hh in range(heads):
        o_ref[0, :, hh * V_DIM:(hh + 1) * V_DIM] = (acc_scr[hh] / carry[2 * hh + 1]).T.astype(BF16)


def _flash_attn(q, k, vt, *, bq, heads):
    B, H, S, _ = q.shape
    return pl.pallas_call(
        functools.partial(_flash_kernel, bq=bq, bk=bq, heads=heads),
        grid=(B, H // heads, S // bq),
        in_specs=[pl.BlockSpec((1, heads, bq, QK_PAD), lambda b, h, i: (b, h, i, 0)),
                  pl.BlockSpec((1, heads, S, QK_PAD), lambda b, h, i: (b, h, 0, 0)),
                  pl.BlockSpec((1, heads, V_DIM, S), lambda b, h, i: (b, h, 0, 0))],
        out_specs=pl.BlockSpec((1, bq, heads * V_DIM), lambda b, h, i: (b, i, h)),
        out_shape=jax.ShapeDtypeStruct((B, S, H * V_DIM), BF16),
        scratch_shapes=[pltpu.VMEM((heads, V_DIM, bq), F32)],
        compiler_params=_cparams(("parallel", "parallel", "arbitrary")),
        name="flash_attn",
    )(q, k, vt)


def _mix_kernel(hc_ref, hp_ref, o_ref, gates_ref, x_ref, wdw_ref, bdw_ref, lng_ref, lnb_ref,
                wco_ref, wao_ref, wo_ref, h1_ref, hbuf, cbuf, *, tm, blocks_per_seq, rows):
    i = pl.program_id(0)
    first = (i % blocks_per_seq) == 0
    hbuf[0, 0:HALO, :] = jnp.where(first, 0.0, hp_ref[...])
    hbuf[0, HALO:HALO + tm, :] = hc_ref[...]
    D = hc_ref.shape[1]
    span = HALO + tm - SUB
    for s in range(1, SUB):
        hbuf[s, 0:span, :] = hbuf[0, s:s + span, :]
    base = HALO - (CONV_WIDTH - 1)
    for c in range(D // LANE):
        cs = slice(c * LANE, (c + 1) * LANE)
        for r in range(tm // rows):
            acc = jnp.broadcast_to(bdw_ref[:, cs], (rows, LANE))
            for kk in range(CONV_WIDTH):
                shift = (base + kk) % SUB
                start = r * rows + base + kk - shift
                acc = acc + wdw_ref[kk:kk + 1, cs] * hbuf[shift, start:start + rows, cs]
            cbuf[r * rows:(r + 1) * rows, cs] = acc
    y = cbuf[...]
    mu = jnp.mean(y, axis=-1, keepdims=True)
    yc = y - mu
    var = jnp.mean(yc * yc, axis=-1, keepdims=True)
    yn = yc * lax.rsqrt(var + EPS) * lng_ref[...] + lnb_ref[...]
    act = (yn * _sigmoid(yn)).astype(BF16)
    y_conv = _dot(act, wco_ref[...])
    y_attn = _dot(o_ref[...], wao_ref[...])
    gts = gates_ref[...].astype(F32)
    mixed = (gts[:, :D] * y_conv + gts[:, D:] * y_attn).astype(BF16)
    h1_ref[...] = x_ref[...] + _dot(mixed, wo_ref[...])


def _mix(hglu, o2, gates, x2, wdw, bdw, lng, lnb, wco, wao, wo, *, S, tm):
    T, D = x2.shape
    full = lambda a: pl.BlockSpec(a.shape, lambda i: (0, 0))
    row = lambda n: pl.BlockSpec((tm, n), lambda i: (i, 0))
    per = tm // HALO
    halo = pl.BlockSpec((HALO, D), lambda i: (jnp.maximum(i * per - 1, 0), 0))
    return pl.pallas_call(
        functools.partial(_mix_kernel, tm=tm, blocks_per_seq=S // tm, rows=64),
        grid=(T // tm,),
        in_specs=[row(D), halo, row(D), row(2 * D), row(D), full(wdw), full(bdw), full(lng), full(lnb),
                  full(wco), full(wao), full(wo)],
        out_specs=row(D),
        out_shape=jax.ShapeDtypeStruct((T, D), F32),
        scratch_shapes=[pltpu.VMEM((SUB, HALO + tm, D), F32), pltpu.VMEM((tm, D), F32)],
        compiler_params=_cparams(("parallel",)),
        name="mix",
    )(hglu, hglu, o2, gates, x2, wdw, bdw, lng, lnb, wco, wao, wo)


def _peer_prep_kernel(h1_ref, g_ref, wpqt_ref, sk_ref, xnt_ref, st_ref):
    tm = h1_ref.shape[0]
    xn32 = _rms(h1_ref[...], g_ref[...])
    xnt_ref[...] = xn32.T.astype(BF16)
    xn = xn32.astype(BF16)
    qt = _dot_nt(wpqt_ref[...], xn).astype(BF16)
    groups = N_KEYS // SUB
    for hp in range(2 * PEER_HEADS):
        rs = slice(hp * N_KEYS, (hp + 1) * N_KEYS)
        s = _dot(sk_ref[hp], qt[rs, :])
        for a in range(tm // LANE):
            st_ref[hp * groups:(hp + 1) * groups, a * SUB:(a + 1) * SUB, :] = (
                s[:, a * LANE:(a + 1) * LANE].reshape(groups, SUB, LANE))


def _peer_prep(h1, g, wpqt, sk, *, tm):
    T, D = h1.shape
    G = wpqt.shape[0] // SUB
    return pl.pallas_call(
        _peer_prep_kernel,
        grid=(T // tm,),
        in_specs=[pl.BlockSpec((tm, D), lambda i: (i, 0)), pl.BlockSpec(g.shape, lambda i: (0, 0)),
                  pl.BlockSpec(wpqt.shape, lambda i: (0, 0)), pl.BlockSpec(sk.shape, lambda i: (0, 0, 0))],
        out_specs=[pl.BlockSpec((D, tm), lambda i: (0, i)),
                   pl.BlockSpec((G, tm // LANE * SUB, LANE), lambda i: (0, i, 0))],
        out_shape=[jax.ShapeDtypeStruct((D, T), BF16), jax.ShapeDtypeStruct((G, T // LANE * SUB, LANE), F32)],
        compiler_params=_cparams(("parallel",)),
        name="peer_prep",
    )(h1, g, wpqt, sk)


def _tree(op, xs):
    xs = list(xs)
    while len(xs) > 1:
        xs = [op(xs[i], xs[i + 1]) for i in range(0, len(xs) - 1, 2)] + ([xs[-1]] if len(xs) % 2 else [])
    return xs[0]


def _first_index(xs, m, chains):
    n = len(xs)
    per = n // chains
    heads = []
    for c in range(chains):
        idx = jnp.full(m.shape, float(n), F32)
        for i in reversed(range(c * per, (c + 1) * per)):
            idx = jnp.where(xs[i] == m, float(i), idx)
        heads.append(idx)
    return _tree(jnp.minimum, heads)


def _peer_topk_kernel(st_ref, r2_ref, lc_ref, e1_ref, e2_ref, work, rank2, vals, idx1):
    K = PEER_TOPK

    def rows(k):
        return k // SUB, pl.ds(k % SUB, SUB, stride=SUB)

    def score(p, k):
        g, r = rows(p * N_KEYS + k)
        return st_ref[g, r, :]

    for p in range(2):
        for k in range(N_KEYS):
            work[p, k] = score(p, k)
    for k in range(N_KEYS):
        rank2[k] = jnp.full((SUB, LANE), float(K), F32)

    def extract(a, carry):
        af = lax.convert_element_type(a, F32)
        for p in range(2):
            w = [work[p, k] for k in range(N_KEYS)]
            m = _tree(jnp.maximum, w)
            idx = _first_index(w, m, 8)
            vals[p, a] = m
            if p == 0:
                idx1[a] = idx
            for k in range(N_KEYS):
                sel = idx == float(k)
                work[p, k] = jnp.where(sel, -jnp.inf, w[k])
                if p == 1:
                    rank2[k] = jnp.where(sel, af, rank2[k])
        return carry

    lax.fori_loop(0, K, extract, 0)

    v1 = [vals[0, a] for a in range(K)]
    v2 = [vals[1, b] for b in range(K)]
    plen = [jnp.zeros((SUB, LANE), F32) for _ in range(K)]
    front = [v1[a] + v2[0] for a in range(K)]
    for _ in range(K):
        m = _tree(jnp.maximum, front)
        a_star = _first_index(front, m, 2)
        lsel = jnp.zeros((SUB, LANE), F32)
        v1sel = jnp.zeros((SUB, LANE), F32)
        sels = []
        for a in range(K):
            sel = a_star == float(a)
            sels.append(sel)
            plen[a] = plen[a] + jnp.where(sel, 1.0, 0.0)
            lsel = jnp.where(sel, plen[a], lsel)
            v1sel = jnp.where(sel, v1[a], v1sel)
        v2next = jnp.zeros((SUB, LANE), F32)
        for b in range(1, K):
            v2next = jnp.where(lsel == float(b), v2[b], v2next)
        fnew = jnp.where(lsel >= float(K), -jnp.inf, v1sel + v2next)
        for a in range(K):
            front[a] = jnp.where(sels[a], fnew, front[a])
    e1s = [jnp.exp(v1[a] - v1[0]) for a in range(K)]
    z = jnp.zeros((SUB, LANE), F32)
    for b in range(K):
        inner = jnp.zeros((SUB, LANE), F32)
        for a in range(K):
            inner = inner + jnp.where(plen[a] > float(b), e1s[a], 0.0)
        z = z + jnp.exp(v2[b] - v2[0]) * inner
    rz = 1.0 / z
    ids = [idx1[a] for a in range(K)]
    for k in range(N_KEYS):
        g, r = rows(k)
        lc = jnp.zeros((SUB, LANE), F32)
        for a in range(K):
            lc = jnp.where(ids[a] == float(k), plen[a], lc)
        lc_ref[0, g, r, :] = lc
        e1_ref[0, g, r, :] = jnp.exp(score(0, k) - v1[0]) * rz
        r2_ref[0, g, r, :] = rank2[k]
        e2_ref[0, g, r, :] = jnp.exp(score(1, k) - v2[0])


def _peer_topk(st, *, T, tokens):
    groups = N_KEYS // SUB
    rows_per_step = tokens // LANE * SUB
    tab = pl.BlockSpec((1, groups, rows_per_step, LANE), lambda t, h: (h, 0, t, 0))
    shp = jax.ShapeDtypeStruct((PEER_HEADS, groups, T // LANE * SUB, LANE), F32)
    vreg = (SUB, LANE)
    return pl.pallas_call(
        _peer_topk_kernel,
        grid=(T // tokens, PEER_HEADS),
        in_specs=[pl.BlockSpec((2 * groups, rows_per_step, LANE), lambda t, h: (h, t, 0))],
        out_specs=[tab, tab, tab, tab],
        out_shape=[shp, shp, shp, shp],
        scratch_shapes=[pltpu.VMEM((2, N_KEYS) + vreg, F32), pltpu.VMEM((N_KEYS,) + vreg, F32),
                        pltpu.VMEM((2, PEER_TOPK) + vreg, F32), pltpu.VMEM((PEER_TOPK,) + vreg, F32)],
        compiler_params=_cparams(("parallel", "parallel")),
        name="peer_topk",
    )(st)


def _gelu(a):
    return 0.5 * a * (1.0 + lax.erf(a * (1.0 / math.sqrt(2.0))))


def _interleave(major, minor):
    out, done = [], 0
    for n, item in enumerate(major):
        out.append(item)
        want = (n + 1) * len(minor) // len(major)
        out += minor[done:want]
        done = want
    return out


def _peer_dense_kernel(xnt_ref, u_ref, vt_ref, r2_ref, lc_ref, e1_ref, e2_ref, out_ref, r2b, e2b, w_scr, act_scr,
                       *, nchunks, nwork):
    s = pl.program_id(0)
    D, tb = xnt_ref.shape
    ec = u_ref.shape[0]
    tiles = ec // N_KEYS
    nl = tb // LANE
    packed = (N_KEYS // BF16_ROWS, BF16_ROWS, LANE)
    w_item = jnp.clip(s - 1, 0, nwork - 1)
    c_item = jnp.clip(s - 2, 0, nwork - 1)
    chunk = w_item % nchunks

    @pl.when(s == 0)
    def _():
        w_scr[...] = jnp.zeros(w_scr.shape, BF16)
        act_scr[...] = jnp.zeros(act_scr.shape, F32)

    @pl.when(chunk == 0)
    def _():
        for h in range(PEER_HEADS):
            for c in range(nl):
                r2b[h, c] = r2_ref[h, :, c * SUB:(c + 1) * SUB, :].reshape(packed).astype(BF16)
                e2b[h, c] = e2_ref[h, :, c * SUB:(c + 1) * SUB, :].reshape(packed).astype(BF16)

    @pl.when(c_item % nchunks == 0)
    def _():
        out_ref[...] = jnp.zeros(out_ref.shape, F32)

    cur = s % 2
    prev = 1 - cur
    zero = jnp.zeros((), BF16)
    halves = 2
    hw = tb // halves
    dp = D // halves
    share = 2

    def scores(n):
        cs = slice(n * hw, (n + 1) * hw)
        act_scr[cur, :, cs] = _dot(u_ref[...], xnt_ref[:, cs])

    def combine(n):
        rs = slice(n * dp, (n + 1) * dp)
        out_ref[rs, :] += _dot(vt_ref[rs, :], w_scr[cur])

    def weights(i0, c):
        gs = [jnp.zeros(packed, BF16) for _ in range(share)]
        for h in range(PEER_HEADS):
            r2 = r2b[h, c]
            e2 = e2b[h, c]
            for n in range(share):
                row = c * SUB + i0 + n
                lc = jnp.broadcast_to(lc_ref[h, pl.ds(chunk, 1), row, :], (BF16_ROWS, LANE)).astype(BF16)
                e1 = jnp.broadcast_to(e1_ref[h, pl.ds(chunk, 1), row, :], (BF16_ROWS, LANE)).astype(BF16)
                gs[n] = gs[n] + jnp.where(r2 < lc[None], e2, zero) * e1[None]
        cs = slice(c * LANE, (c + 1) * LANE)
        for n in range(share):
            rs = slice((i0 + n) * N_KEYS, (i0 + n + 1) * N_KEYS)
            a = _gelu(act_scr[prev, rs, cs]).astype(BF16)
            w_scr[prev, rs, cs] = gs[n].reshape(N_KEYS, LANE) * a

    mxu = [functools.partial(combine, 0), functools.partial(scores, 0),
           functools.partial(combine, 1), functools.partial(scores, 1)]
    vpu = [functools.partial(weights, i0, c) for i0 in range(0, tiles, share) for c in range(nl)]
    for stage in _interleave(mxu, vpu):
        stage()


def _peer_dense(xnt, u, vt, r2, lc, e1, e2, *, tb, ec):
    D, T = xnt.shape
    E = u.shape[0]
    assert ec == N_KEYS * SUB
    nchunks = E // ec
    nwork = (T // tb) * nchunks
    groups = N_KEYS // SUB
    nl = tb // LANE
    s_item = lambda s: jnp.minimum(s, nwork - 1)
    w_item = lambda s: jnp.clip(s - 1, 0, nwork - 1)
    c_item = lambda s: jnp.clip(s - 2, 0, nwork - 1)
    tab = pl.BlockSpec((PEER_HEADS, groups, nl * SUB, LANE), lambda s: (0, 0, w_item(s) // nchunks, 0))
    packed = (PEER_HEADS, nl, N_KEYS // BF16_ROWS, BF16_ROWS, LANE)
    return pl.pallas_call(
        functools.partial(_peer_dense_kernel, nchunks=nchunks, nwork=nwork),
        grid=(nwork + 2,),
        in_specs=[pl.BlockSpec((D, tb), lambda s: (0, s_item(s) // nchunks)),
                  pl.BlockSpec((ec, D), lambda s: (s_item(s) % nchunks, 0)),
                  pl.BlockSpec((D, ec), lambda s: (0, c_item(s) % nchunks)),
                  tab, tab, tab, tab],
        out_specs=pl.BlockSpec((D, tb), lambda s: (0, c_item(s) // nchunks)),
        out_shape=jax.ShapeDtypeStruct((D, T), F32),
        scratch_shapes=[pltpu.VMEM(packed, BF16), pltpu.VMEM(packed, BF16), pltpu.VMEM((2, ec, tb), BF16),
                        pltpu.VMEM((2, ec, tb), F32)],
        compiler_params=_cparams(("arbitrary",)),
        name="peer_dense",
    )(xnt, u, vt, r2, lc, e1, e2)


def _final_kernel(pt_ref, h1_ref, g_ref, out_ref, *, normalize):
    h2 = h1_ref[...] + pt_ref[...].T
    out_ref[...] = _rms(h2, g_ref[...]) if normalize else h2


def _final(pt, h1, g, *, tb, normalize):
    T, D = h1.shape
    return pl.pallas_call(
        functools.partial(_final_kernel, normalize=normalize),
        grid=(T // tb,),
        in_specs=[pl.BlockSpec((D, tb), lambda i: (0, i)), pl.BlockSpec((tb, D), lambda i: (i, 0)),
                  pl.BlockSpec(g.shape, lambda i: (0, 0))],
        out_specs=pl.BlockSpec((tb, D), lambda i: (i, 0)),
        out_shape=jax.ShapeDtypeStruct((T, D), F32),
        compiler_params=_cparams(("parallel",)),
        name="final_norm",
    )(pt, h1, g)


def _rot_cols(w):
    half = w.shape[-1] // 2
    return jnp.concatenate([-w[..., half:], w[..., :half]], axis=-1)


def _layer(h2d, pos2, invf, p, *, B, S):
    T, D = h2d.shape
    w_in = p["w_in"]
    o0, o1, o2, o3 = D, 2 * D, 2 * D + Q_LORA, 2 * D + Q_LORA + KV_LORA + QK_ROPE
    w_kr = w_in[:, o2 + KV_LORA:o3]
    zpad = jnp.zeros((D, LANE - QK_ROPE), F32)
    wkv = jnp.concatenate([w_in[:, o2:o2 + KV_LORA], w_kr, zpad, _rot_cols(w_kr), zpad], axis=1)
    hglu, cq, ckv, gates = _in_proj(
        h2d, p["mix_norm_g"][None, :], w_in[:, :o0].astype(BF16), w_in[:, o0:o1].astype(BF16),
        w_in[:, o1:o2].astype(BF16), wkv.astype(BF16), w_in[:, o3:].astype(BF16), p["b_gate"][None, :], tm=512)

    w_uq = p["w_uq"].reshape(Q_LORA, N_HEADS, QK_NOPE + QK_ROPE)
    zq = jnp.zeros((Q_LORA, N_HEADS, LANE - QK_ROPE), F32)
    wqa = jnp.concatenate([w_uq, zq], axis=-1).reshape(Q_LORA, N_HEADS * QK_PAD)
    wqr = jnp.concatenate([_rot_cols(w_uq[..., QK_NOPE:]), zq], axis=-1).reshape(Q_LORA, N_HEADS * LANE)
    w_ukv = p["w_ukv"].reshape(KV_LORA, N_HEADS, QK_NOPE + V_DIM)
    wk = w_ukv[..., :QK_NOPE].reshape(KV_LORA, N_HEADS * QK_NOPE)
    wv = w_ukv[..., QK_NOPE:].reshape(KV_LORA, N_HEADS * V_DIM)
    q, k, v = _mla_proj(cq, ckv, pos2, invf, p["q_norm_g"][None, :], wqa.astype(BF16), wqr.astype(BF16),
                        p["kv_norm_g"][None, :], wk.astype(BF16), wv.T.astype(BF16), B=B, S=S, tm=512)
    o = _flash_attn(q, k, v, bq=512, heads=4)

    h1 = _mix(hglu, o.reshape(T, D), gates, h2d, p["w_dw"].reshape(CONV_WIDTH, D), p["b_dw"][None, :],
              p["conv_ln_g"][None, :], p["conv_ln_b"][None, :], p["w_conv_out"].astype(BF16),
              p["w_attn_out"].astype(BF16), p["w_o"].astype(BF16), S=S, tm=256)

    sk = p["peer_sub_keys"].reshape(2 * PEER_HEADS, N_KEYS, -1).astype(BF16)
    xnt, st = _peer_prep(h1, p["ffn_norm_g"][None, :], p["w_peer_q"].T.astype(BF16), sk, tm=512)
    r2, lc, e1, e2 = _peer_topk(st, T=T, tokens=1024)
    pt = _peer_dense(xnt, p["peer_u"].astype(BF16), p["peer_v"].T.astype(BF16), r2, lc, e1, e2, tb=512, ec=1024)
    return h1, pt


def kernel(x, positions, mix_norm_g, w_in, b_gate, w_dw, b_dw, conv_ln_g, conv_ln_b, w_conv_out, q_norm_g, w_uq,
           kv_norm_g, w_ukv, w_attn_out, w_o, ffn_norm_g, w_peer_q, peer_sub_keys, peer_u, peer_v, final_norm_g):
    B, S, D = x.shape
    T = B * S
    stacked = dict(mix_norm_g=mix_norm_g, w_in=w_in, b_gate=b_gate, w_dw=w_dw, b_dw=b_dw, conv_ln_g=conv_ln_g,
                   conv_ln_b=conv_ln_b, w_conv_out=w_conv_out, q_norm_g=q_norm_g, w_uq=w_uq, kv_norm_g=kv_norm_g,
                   w_ukv=w_ukv, w_attn_out=w_attn_out, w_o=w_o, ffn_norm_g=ffn_norm_g, w_peer_q=w_peer_q,
                   peer_sub_keys=peer_sub_keys, peer_u=peer_u, peer_v=peer_v)
    depth = w_in.shape[0]
    half = QK_ROPE // 2
    inv_freq = ROPE_BASE ** (-jnp.arange(half, dtype=F32) / half)
    invf = jnp.concatenate([inv_freq, inv_freq, jnp.zeros((LANE - QK_ROPE,), F32)])[None, :]
    pos2 = positions.reshape(T, 1)
    h = x.reshape(T, D)
    for l in range(depth):
        p = {name: a[l] for name, a in stacked.items()}
        h1, pt = _layer(h, pos2, invf, p, B=B, S=S)
        last = l + 1 == depth
        h = _final(pt, h1, final_norm_g[None, :], tb=512, normalize=last)
    return h.reshape(B, S, D)
```

```python
import functools
import math

import jax
import jax.numpy as jnp
from jax import lax
from jax.experimental import pallas as pl
from jax.experimental.pallas import tpu as pltpu

F32 = jnp.float32
BF16 = jnp.bfloat16

EPS = 1e-6
CONV_WIDTH = 31
N_HEADS = 8
QK_NOPE = 128
QK_ROPE = 64
V_DIM = 128
Q_LORA = 256
KV_LORA = 128
ROPE_BASE = 10000.0
N_KEYS = 128
PEER_HEADS = 8
PEER_TOPK = 16
QK_PAD = 256
LANE = 128
SUB = 8
BF16_ROWS = 16
HALO = 32
VMEM_LIMIT = 56 * 1024 * 1024


def _cparams(sem):
    return pltpu.CompilerParams(dimension_semantics=sem, vmem_limit_bytes=VMEM_LIMIT)


def _dot(a, b):
    return jnp.dot(a, b, preferred_element_type=F32)


def _dot_nt(a, b):
    return lax.dot_general(a, b, (((1,), (1,)), ((), ())), preferred_element_type=F32)


def _sigmoid(x):
    return 1.0 / (1.0 + jnp.exp(-x))


def _rms(x, g):
    return x * lax.rsqrt(jnp.mean(x * x, axis=-1, keepdims=True) + EPS) * g


def _inproj_kernel(x_ref, g_ref, wv_ref, wg_ref, wq_ref, wkv_ref, wgl_ref, bg_ref,
                   hglu_ref, cq_ref, ckv_ref, gates_ref):
    a = _rms(x_ref[...], g_ref[...]).astype(BF16)
    val = _dot(a, wv_ref[...])
    gate = _dot(a, wg_ref[...])
    hglu_ref[...] = val * _sigmoid(gate)
    cq_ref[...] = _dot(a, wq_ref[...])
    ckv_ref[...] = _dot(a, wkv_ref[...])
    gates_ref[...] = _sigmoid(_dot(a, wgl_ref[...]) + bg_ref[...]).astype(BF16)


def _in_proj(x2, g, wv, wg, wq, wkv, wgl, bg, *, tm):
    T, D = x2.shape
    full = lambda a: pl.BlockSpec(a.shape, lambda i: (0, 0))
    row = lambda n: pl.BlockSpec((tm, n), lambda i: (i, 0))
    return pl.pallas_call(
        _inproj_kernel,
        grid=(T // tm,),
        in_specs=[row(D), full(g), full(wv), full(wg), full(wq), full(wkv), full(wgl), full(bg)],
        out_specs=[row(wv.shape[1]), row(wq.shape[1]), row(wkv.shape[1]), row(wgl.shape[1])],
        out_shape=[jax.ShapeDtypeStruct((T, wv.shape[1]), F32),
                   jax.ShapeDtypeStruct((T, wq.shape[1]), F32),
                   jax.ShapeDtypeStruct((T, wkv.shape[1]), F32),
                   jax.ShapeDtypeStruct((T, wgl.shape[1]), BF16)],
        compiler_params=_cparams(("parallel",)),
        name="in_proj",
    )(x2, g, wv, wg, wq, wkv, wgl, bg)


def _mla_proj_kernel(cq_ref, ckv_ref, pos_ref, invf_ref, qg_ref, wqa_ref, wqr_ref, kvg_ref, wk_ref, wv_ref,
                     q_ref, k_ref, v_ref, *, scale):
    qn = _rms(cq_ref[...], qg_ref[...]).astype(BF16)
    ckv_full = ckv_ref[...]
    kvn = _rms(ckv_full[:, :KV_LORA], kvg_ref[...]).astype(BF16)
    ang = pos_ref[...].astype(F32) * invf_ref[...]
    cosv = jnp.cos(ang)
    sinv = jnp.sin(ang)
    kr = (ckv_full[:, KV_LORA:KV_LORA + LANE] * cosv + ckv_full[:, KV_LORA + LANE:] * sinv).astype(BF16)
    qa = _dot(qn, wqa_ref[...])
    qr = _dot(qn, wqr_ref[...])
    kn = _dot(kvn, wk_ref[...])
    vt = _dot_nt(wv_ref[...], kvn)
    for h in range(N_HEADS):
        q_ref[0, h, :, 0:LANE] = (qa[:, h * QK_PAD:h * QK_PAD + LANE] * scale).astype(BF16)
        q_rope = qa[:, h * QK_PAD + LANE:(h + 1) * QK_PAD] * cosv + qr[:, h * LANE:(h + 1) * LANE] * sinv
        q_ref[0, h, :, LANE:QK_PAD] = (q_rope * scale).astype(BF16)
        k_ref[0, h, :, 0:LANE] = kn[:, h * LANE:(h + 1) * LANE].astype(BF16)
        k_ref[0, h, :, LANE:QK_PAD] = kr
        v_ref[0, h, :, :] = vt[h * V_DIM:(h + 1) * V_DIM, :].astype(BF16)


def _mla_proj(cq, ckv, pos2, invf, qg, wqa, wqr, kvg, wk, wv, *, B, S, tm):
    T = cq.shape[0]
    nb = S // tm
    full = lambda a: pl.BlockSpec(a.shape, lambda i: (0, 0))
    row = lambda n: pl.BlockSpec((tm, n), lambda i: (i, 0))
    head = lambda n: pl.BlockSpec((1, N_HEADS, tm, n), lambda i: (i // nb, 0, i % nb, 0))
    head_t = pl.BlockSpec((1, N_HEADS, V_DIM, tm), lambda i: (i // nb, 0, 0, i % nb))
    scale = math.log2(math.e) / math.sqrt(QK_NOPE + QK_ROPE)
    return pl.pallas_call(
        functools.partial(_mla_proj_kernel, scale=scale),
        grid=(T // tm,),
        in_specs=[row(cq.shape[1]), row(ckv.shape[1]), row(1), full(invf), full(qg), full(wqa), full(wqr),
                  full(kvg), full(wk), full(wv)],
        out_specs=[head(QK_PAD), head(QK_PAD), head_t],
        out_shape=[jax.ShapeDtypeStruct((B, N_HEADS, S, QK_PAD), BF16),
                   jax.ShapeDtypeStruct((B, N_HEADS, S, QK_PAD), BF16),
                   jax.ShapeDtypeStruct((B, N_HEADS, V_DIM, S), BF16)],
        compiler_params=_cparams(("parallel",)),
        name="mla_proj",
    )(cq, ckv, pos2, invf, qg, wqa, wqr, kvg, wk, wv)


def _flash_kernel(q_ref, k_ref, vt_ref, o_ref, acc_scr, *, bq, bk, heads):
    qi = pl.program_id(2)
    acc_scr[...] = jnp.zeros(acc_scr.shape, F32)

    def step(ki, carry, diagonal):
        off = pl.multiple_of(ki * bk, bk)
        out = []
        for hh in range(heads):
            m_prev, l_prev = carry[2 * hh], carry[2 * hh + 1]
            st = _dot_nt(k_ref[0, hh, pl.ds(off, bk), :], q_ref[0, hh])
            if diagonal:
                kpos = lax.broadcasted_iota(jnp.int32, (bk, bq), 0)
                qpos = lax.broadcasted_iota(jnp.int32, (bk, bq), 1)
                st = jnp.where(kpos <= qpos, st, -1e30)
            m_new = jnp.maximum(m_prev, jnp.max(st, axis=0, keepdims=True))
            p = jnp.exp2(st - m_new)
            alpha = jnp.exp2(m_prev - m_new)
            l_new = alpha * l_prev + jnp.sum(p, axis=0, keepdims=True)
            pv = _dot(vt_ref[0, hh, :, pl.ds(off, bk)], p.astype(BF16))
            acc_scr[hh] = alpha * acc_scr[hh] + pv
            out += [m_new, l_new]
        return tuple(out)

    init = (jnp.full((1, bq), -1e30, F32), jnp.zeros((1, bq), F32)) * heads
    carry = lax.fori_loop(0, qi, lambda ki, c: step(ki, c, False), init)
    carry = step(qi, carry, True)
    for hh in range(heads):
        o_ref[0, :, hh * V_DIM:(hh + 1) * V_DIM] = (acc_scr[hh] / carry[2 * hh + 1]).T.astype(BF16)


def _flash_attn(q, k, vt, *, bq, heads):
    B, H, S, _ = q.shape
    return pl.pallas_call(
        functools.partial(_flash_kernel, bq=bq, bk=bq, heads=heads),
        grid=(B, H // heads, S // bq),
        in_specs=[pl.BlockSpec((1, heads, bq, QK_PAD), lambda b, h, i: (b, h, i, 0)),
                  pl.BlockSpec((1, heads, S, QK_PAD), lambda b, h, i: (b, h, 0, 0)),
                  pl.BlockSpec((1, heads, V_DIM, S), lambda b, h, i: (b, h, 0, 0))],
        out_specs=pl.BlockSpec((1, bq, heads * V_DIM), lambda b, h, i: (b, i, h)),
        out_shape=jax.ShapeDtypeStruct((B, S, H * V_DIM), BF16),
        scratch_shapes=[pltpu.VMEM((heads, V_DIM, bq), F32)],
        compiler_params=_cparams(("parallel", "parallel", "arbitrary")),
        name="flash_attn",
    )(q, k, vt)


def _mix_kernel(hc_ref, hp_ref, o_ref, gates_ref, x_ref, wdw_ref, bdw_ref, lng_ref, lnb_ref,
                wco_ref, wao_ref, wo_ref, h1_ref, hbuf, cbuf, *, tm, blocks_per_seq, rows):
    i = pl.program_id(0)
    first = (i % blocks_per_seq) == 0
    hbuf[0, 0:HALO, :] = jnp.where(first, 0.0, hp_ref[...])
    hbuf[0, HALO:HALO + tm, :] = hc_ref[...]
    D = hc_ref.shape[1]
    span = HALO + tm - SUB
    for s in range(1, SUB):
        hbuf[s, 0:span, :] = hbuf[0, s:s + span, :]
    base = HALO - (CONV_WIDTH - 1)
    for c in range(D // LANE):
        cs = slice(c * LANE, (c + 1) * LANE)
        for r in range(tm // rows):
            acc = jnp.broadcast_to(bdw_ref[:, cs], (rows, LANE))
            for kk in range(CONV_WIDTH):
                shift = (base + kk) % SUB
                start = r * rows + base + kk - shift
                acc = acc + wdw_ref[kk:kk + 1, cs] * hbuf[shift, start:start + rows, cs]
            cbuf[r * rows:(r + 1) * rows, cs] = acc
    y = cbuf[...]
    mu = jnp.mean(y, axis=-1, keepdims=True)
    yc = y - mu
    var = jnp.mean(yc * yc, axis=-1, keepdims=True)
    yn = yc * lax.rsqrt(var + EPS) * lng_ref[...] + lnb_ref[...]
    act = (yn * _sigmoid(yn)).astype(BF16)
    y_conv = _dot(act, wco_ref[...])
    y_attn = _dot(o_ref[...], wao_ref[...])
    gts = gates_ref[...].astype(F32)
    mixed = (gts[:, :D] * y_conv + gts[:, D:] * y_attn).astype(BF16)
    h1_ref[...] = x_ref[...] + _dot(mixed, wo_ref[...])


def _mix(hglu, o2, gates, x2, wdw, bdw, lng, lnb, wco, wao, wo, *, S, tm):
    T, D = x2.shape
    full = lambda a: pl.BlockSpec(a.shape, lambda i: (0, 0))
    row = lambda n: pl.BlockSpec((tm, n), lambda i: (i, 0))
    per = tm // HALO
    halo = pl.BlockSpec((HALO, D), lambda i: (jnp.maximum(i * per - 1, 0), 0))
    return pl.pallas_call(
        functools.partial(_mix_kernel, tm=tm, blocks_per_seq=S // tm, rows=64),
        grid=(T // tm,),
        in_specs=[row(D), halo, row(D), row(2 * D), row(D), full(wdw), full(bdw), full(lng), full(lnb),
                  full(wco), full(wao), full(wo)],
        out_specs=row(D),
        out_shape=jax.ShapeDtypeStruct((T, D), F32),
        scratch_shapes=[pltpu.VMEM((SUB, HALO + tm, D), F32), pltpu.VMEM((tm, D), F32)],
        compiler_params=_cparams(("parallel",)),
        name="mix",
    )(hglu, hglu, o2, gates, x2, wdw, bdw, lng, lnb, wco, wao, wo)


def _peer_prep_kernel(h1_ref, g_ref, wpqt_ref, sk_ref, xnt_ref, st_ref):
    tm = h1_ref.shape[0]
    xn32 = _rms(h1_ref[...], g_ref[...])
    xnt_ref[...] = xn32.T.astype(BF16)
    xn = xn32.astype(BF16)
    qt = _dot_nt(wpqt_ref[...], xn).astype(BF16)
    groups = N_KEYS // SUB
    for hp in range(2 * PEER_HEADS):
        rs = slice(hp * N_KEYS, (hp + 1) * N_KEYS)
        s = _dot(sk_ref[hp], qt[rs, :])
        for a in range(tm // LANE):
            st_ref[hp * groups:(hp + 1) * groups, a * SUB:(a + 1) * SUB, :] = (
                s[:, a * LANE:(a + 1) * LANE].reshape(groups, SUB, LANE))


def _peer_prep(h1, g, wpqt, sk, *, tm):
    T, D = h1.shape
    G = wpqt.shape[0] // SUB
    return pl.pallas_call(
        _peer_prep_kernel,
        grid=(T // tm,),
        in_specs=[pl.BlockSpec((tm, D), lambda i: (i, 0)), pl.BlockSpec(g.shape, lambda i: (0, 0)),
                  pl.BlockSpec(wpqt.shape, lambda i: (0, 0)), pl.BlockSpec(sk.shape, lambda i: (0, 0, 0))],
        out_specs=[pl.BlockSpec((D, tm), lambda i: (0, i)),
                   pl.BlockSpec((G, tm // LANE * SUB, LANE), lambda i: (0, i, 0))],
        out_shape=[jax.ShapeDtypeStruct((D, T), BF16), jax.ShapeDtypeStruct((G, T // LANE * SUB, LANE), F32)],
        compiler_params=_cparams(("parallel",)),
        name="peer_prep",
    )(h1, g, wpqt, sk)


def _tree(op, xs):
    xs = list(xs)
    while len(xs) > 1:
        xs = [op(xs[i], xs[i + 1]) for i in range(0, len(xs) - 1, 2)] + ([xs[-1]] if len(xs) % 2 else [])
    return xs[0]


def _first_index(xs, m, chains):
    n = len(xs)
    per = n // chains
    heads = []
    for c in range(chains):
        idx = jnp.full(m.shape, float(n), F32)
        for i in reversed(range(c * per, (c + 1) * per)):
            idx = jnp.where(xs[i] == m, float(i), idx)
        heads.append(idx)
    return _tree(jnp.minimum, heads)


def _peer_topk_kernel(st_ref, r2_ref, lc_ref, e1_ref, e2_ref, work, rank2, vals, idx1):
    K = PEER_TOPK

    def rows(k):
        return k // SUB, pl.ds(k % SUB, SUB, stride=SUB)

    def score(p, k):
        g, r = rows(p * N_KEYS + k)
        return st_ref[g, r, :]

    for p in range(2):
        for k in range(N_KEYS):
            work[p, k] = score(p, k)
    for k in range(N_KEYS):
        rank2[k] = jnp.full((SUB, LANE), float(K), F32)

    def extract(a, carry):
        af = lax.convert_element_type(a, F32)
        for p in range(2):
            w = [work[p, k] for k in range(N_KEYS)]
            m = _tree(jnp.maximum, w)
            idx = _first_index(w, m, 8)
            vals[p, a] = m
            if p == 0:
                idx1[a] = idx
            for k in range(N_KEYS):
                sel = idx == float(k)
                work[p, k] = jnp.where(sel, -jnp.inf, w[k])
                if p == 1:
                    rank2[k] = jnp.where(sel, af, rank2[k])
        return carry

    lax.fori_loop(0, K, extract, 0)

    v1 = [vals[0, a] for a in range(K)]
    v2 = [vals[1, b] for b in range(K)]
    plen = [jnp.zeros((SUB, LANE), F32) for _ in range(K)]
    front = [v1[a] + v2[0] for a in range(K)]
    for _ in range(K):
        m = _tree(jnp.maximum, front)
        a_star = _first_index(front, m, 2)
        lsel = jnp.zeros((SUB, LANE), F32)
        v1sel = jnp.zeros((SUB, LANE), F32)
        sels = []
        for a in range(K):
            sel = a_star == float(a)
            sels.append(sel)
            plen[a] = plen[a] + jnp.where(sel, 1.0, 0.0)
            lsel = jnp.where(sel, plen[a], lsel)
            v1sel = jnp.where(sel, v1[a], v1sel)
        v2next = jnp.zeros((SUB, LANE), F32)
        for b in range(1, K):
            v2next = jnp.where(lsel == float(b), v2[b], v2next)
        fnew = jnp.where(lsel >= float(K), -jnp.inf, v1sel + v2next)
        for a in range(K):
            front[a] = jnp.where(sels[a], fnew, front[a])
    e1s = [jnp.exp(v1[a] - v1[0]) for a in range(K)]
    z = jnp.zeros((SUB, LANE), F32)
    for b in range(K):
        inner = jnp.zeros((SUB, LANE), F32)
        for a in range(K):
            inner = inner + jnp.where(plen[a] > float(b), e1s[a], 0.0)
        z = z + jnp.exp(v2[b] - v2[0]) * inner
    rz = 1.0 / z
    ids = [idx1[a] for a in range(K)]
    for k in range(N_KEYS):
        g, r = rows(k)
        lc = jnp.zeros((SUB, LANE), F32)
        for a in range(K):
            lc = jnp.where(ids[a] == float(k), plen[a], lc)
        lc_ref[0, g, r, :] = lc
        e1_ref[0, g, r, :] = jnp.exp(score(0, k) - v1[0]) * rz
        r2_ref[0, g, r, :] = rank2[k]
        e2_ref[0, g, r, :] = jnp.exp(score(1, k) - v2[0])


def _peer_topk(st, *, T, tokens):
    groups = N_KEYS // SUB
    rows_per_step = tokens // LANE * SUB
    tab = pl.BlockSpec((1, groups, rows_per_step, LANE), lambda t, h: (h, 0, t, 0))
    shp = jax.ShapeDtypeStruct((PEER_HEADS, groups, T // LANE * SUB, LANE), F32)
    vreg = (SUB, LANE)
    return pl.pallas_call(
        _peer_topk_kernel,
        grid=(T // tokens, PEER_HEADS),
        in_specs=[pl.BlockSpec((2 * groups, rows_per_step, LANE), lambda t, h: (h, t, 0))],
        out_specs=[tab, tab, tab, tab],
        out_shape=[shp, shp, shp, shp],
        scratch_shapes=[pltpu.VMEM((2, N_KEYS) + vreg, F32), pltpu.VMEM((N_KEYS,) + vreg, F32),
                        pltpu.VMEM((2, PEER_TOPK) + vreg, F32), pltpu.VMEM((PEER_TOPK,) + vreg, F32)],
        compiler_params=_cparams(("parallel", "parallel")),
        name="peer_topk",
    )(st)


def _gelu(a):
    return 0.5 * a * (1.0 + lax.erf(a * (1.0 / math.sqrt(2.0))))


def _interleave(major, minor):
    out, done = [], 0
    for n, item in enumerate(major):
        out.append(item)
        want = (n + 1) * len(minor) // len(major)
        out += minor[done:want]
        done = want
    return out


def _peer_dense_kernel(xnt_ref, u_ref, vt_ref, r2_ref, lc_ref, e1_ref, e2_ref, out_ref, r2b, e2b, w_scr, act_scr,
                       *, nchunks, nwork):
    s = pl.program_id(0)
    D, tb = xnt_ref.shape
    ec = u_ref.shape[0]
    tiles = ec // N_KEYS
    nl = tb // LANE
    packed = (N_KEYS // BF16_ROWS, BF16_ROWS, LANE)
    w_item = jnp.clip(s - 1, 0, nwork - 1)
    c_item = jnp.clip(s - 2, 0, nwork - 1)
    chunk = w_item % nchunks

    @pl.when(s == 0)
    def _():
        w_scr[...] = jnp.zeros(w_scr.shape, BF16)
        act_scr[...] = jnp.zeros(act_scr.shape, F32)

    @pl.when(chunk == 0)
    def _():
        for h in range(PEER_HEADS):
            for c in range(nl):
                r2b[h, c] = r2_ref[h, :, c * SUB:(c + 1) * SUB, :].reshape(packed).astype(BF16)
                e2b[h, c] = e2_ref[h, :, c * SUB:(c + 1) * SUB, :].reshape(packed).astype(BF16)

    @pl.when(c_item % nchunks == 0)
    def _():
        out_ref[...] = jnp.zeros(out_ref.shape, F32)

    cur = s % 2
    prev = 1 - cur
    zero = jnp.zeros((), BF16)
    halves = 2
    hw = tb // halves
    quarters = 4
    dp = D // quarters
    share = 2

    def scores(n):
        cs = slice(n * hw, (n + 1) * hw)
        act_scr[cur, :, cs] = _dot(u_ref[...], xnt_ref[:, cs])

    def combine(n):
        rs = slice(n * dp, (n + 1) * dp)
        out_ref[rs, :] += _dot(vt_ref[rs, :], w_scr[cur])

    def weights(i0, c):
        gs = [jnp.zeros(packed, BF16) for _ in range(share)]
        for h in range(PEER_HEADS):
            r2 = r2b[h, c]
            e2 = e2b[h, c]
            for n in range(share):
                row = c * SUB + i0 + n
                lc = jnp.broadcast_to(lc_ref[h, pl.ds(chunk, 1), row, :], (BF16_ROWS, LANE)).astype(BF16)
                e1 = jnp.broadcast_to(e1_ref[h, pl.ds(chunk, 1), row, :], (BF16_ROWS, LANE)).astype(BF16)
                gs[n] = gs[n] + jnp.where(r2 < lc[None], e2, zero) * e1[None]
        cs = slice(c * LANE, (c + 1) * LANE)
        for n in range(share):
            rs = slice((i0 + n) * N_KEYS, (i0 + n + 1) * N_KEYS)
            a = _gelu(act_scr[prev, rs, cs].astype(BF16))
            w_scr[prev, rs, cs] = gs[n].reshape(N_KEYS, LANE) * a

    mxu = [functools.partial(combine, 0), functools.partial(scores, 0), functools.partial(combine, 1),
           functools.partial(combine, 2), functools.partial(scores, 1), functools.partial(combine, 3)]
    vpu = [functools.partial(weights, i0, c) for i0 in range(0, tiles, share) for c in range(nl)]
    for stage in _interleave(mxu, vpu):
        stage()


def _peer_dense(xnt, u, vt, r2, lc, e1, e2, *, tb, ec):
    D, T = xnt.shape
    E = u.shape[0]
    assert ec == N_KEYS * SUB
    nchunks = E // ec
    nwork = (T // tb) * nchunks
    groups = N_KEYS // SUB
    nl = tb // LANE
    s_item = lambda s: jnp.minimum(s, nwork - 1)
    w_item = lambda s: jnp.clip(s - 1, 0, nwork - 1)
    c_item = lambda s: jnp.clip(s - 2, 0, nwork - 1)
    tab = pl.BlockSpec((PEER_HEADS, groups, nl * SUB, LANE), lambda s: (0, 0, w_item(s) // nchunks, 0))
    packed = (PEER_HEADS, nl, N_KEYS // BF16_ROWS, BF16_ROWS, LANE)
    return pl.pallas_call(
        functools.partial(_peer_dense_kernel, nchunks=nchunks, nwork=nwork),
        grid=(nwork + 2,),
        in_specs=[pl.BlockSpec((D, tb), lambda s: (0, s_item(s) // nchunks)),
                  pl.BlockSpec((ec, D), lambda s: (s_item(s) % nchunks, 0)),
                  pl.BlockSpec((D, ec), lambda s: (0, c_item(s) % nchunks)),
                  tab, tab, tab, tab],
        out_specs=pl.BlockSpec((D, tb), lambda s: (0, c_item(s) // nchunks)),
        out_shape=jax.ShapeDtypeStruct((D, T), F32),
        scratch_shapes=[pltpu.VMEM(packed, BF16), pltpu.VMEM(packed, BF16), pltpu.VMEM((2, ec, tb), BF16),
                        pltpu.VMEM((2, ec, tb), F32)],
        compiler_params=_cparams(("arbitrary",)),
        name="peer_dense",
    )(xnt, u, vt, r2, lc, e1, e2)


def _final_kernel(pt_ref, h1_ref, g_ref, out_ref, *, normalize):
    h2 = h1_ref[...] + pt_ref[...].T
    out_ref[...] = _rms(h2, g_ref[...]) if normalize else h2


def _final(pt, h1, g, *, tb, normalize):
    T, D = h1.shape
    return pl.pallas_call(
        functools.partial(_final_kernel, normalize=normalize),
        grid=(T // tb,),
        in_specs=[pl.BlockSpec((D, tb), lambda i: (0, i)), pl.BlockSpec((tb, D), lambda i: (i, 0)),
                  pl.BlockSpec(g.shape, lambda i: (0, 0))],
        out_specs=pl.BlockSpec((tb, D), lambda i: (i, 0)),
        out_shape=jax.ShapeDtypeStruct((T, D), F32),
        compiler_params=_cparams(("parallel",)),
        name="final_norm",
    )(pt, h1, g)


def _rot_cols(w):
    half = w.shape[-1] // 2
    return jnp.concatenate([-w[..., half:], w[..., :half]], axis=-1)


def _layer(h2d, pos2, invf, p, *, B, S):
    T, D = h2d.shape
    w_in = p["w_in"]
    o0, o1, o2, o3 = D, 2 * D, 2 * D + Q_LORA, 2 * D + Q_LORA + KV_LORA + QK_ROPE
    w_kr = w_in[:, o2 + KV_LORA:o3]
    zpad = jnp.zeros((D, LANE - QK_ROPE), F32)
    wkv = jnp.concatenate([w_in[:, o2:o2 + KV_LORA], w_kr, zpad, _rot_cols(w_kr), zpad], axis=1)
    hglu, cq, ckv, gates = _in_proj(
        h2d, p["mix_norm_g"][None, :], w_in[:, :o0].astype(BF16), w_in[:, o0:o1].astype(BF16),
        w_in[:, o1:o2].astype(BF16), wkv.astype(BF16), w_in[:, o3:].astype(BF16), p["b_gate"][None, :], tm=512)

    w_uq = p["w_uq"].reshape(Q_LORA, N_HEADS, QK_NOPE + QK_ROPE)
    zq = jnp.zeros((Q_LORA, N_HEADS, LANE - QK_ROPE), F32)
    wqa = jnp.concatenate([w_uq, zq], axis=-1).reshape(Q_LORA, N_HEADS * QK_PAD)
    wqr = jnp.concatenate([_rot_cols(w_uq[..., QK_NOPE:]), zq], axis=-1).reshape(Q_LORA, N_HEADS * LANE)
    w_ukv = p["w_ukv"].reshape(KV_LORA, N_HEADS, QK_NOPE + V_DIM)
    wk = w_ukv[..., :QK_NOPE].reshape(KV_LORA, N_HEADS * QK_NOPE)
    wv = w_ukv[..., QK_NOPE:].reshape(KV_LORA, N_HEADS * V_DIM)
    q, k, v = _mla_proj(cq, ckv, pos2, invf, p["q_norm_g"][None, :], wqa.astype(BF16), wqr.astype(BF16),
                        p["kv_norm_g"][None, :], wk.astype(BF16), wv.T.astype(BF16), B=B, S=S, tm=512)
    o = _flash_attn(q, k, v, bq=512, heads=4)

    h1 = _mix(hglu, o.reshape(T, D), gates, h2d, p["w_dw"].reshape(CONV_WIDTH, D), p["b_dw"][None, :],
              p["conv_ln_g"][None, :], p["conv_ln_b"][None, :], p["w_conv_out"].astype(BF16),
              p["w_attn_out"].astype(BF16), p["w_o"].astype(BF16), S=S, tm=256)

    sk = p["peer_sub_keys"].reshape(2 * PEER_HEADS, N_KEYS, -1).astype(BF16)
    xnt, st = _peer_prep(h1, p["ffn_norm_g"][None, :], p["w_peer_q"].T.astype(BF16), sk, tm=512)
    r2, lc, e1, e2 = _peer_topk(st, T=T, tokens=1024)
    pt = _peer_dense(xnt, p["peer_u"].astype(BF16), p["peer_v"].T.astype(BF16), r2, lc, e1, e2, tb=512, ec=1024)
    return h1, pt


def kernel(x, positions, mix_norm_g, w_in, b_gate, w_dw, b_dw, conv_ln_g, conv_ln_b, w_conv_out, q_norm_g, w_uq,
           kv_norm_g, w_ukv, w_attn_out, w_o, ffn_norm_g, w_peer_q, peer_sub_keys, peer_u, peer_v, final_norm_g):
    B, S, D = x.shape
    T = B * S
    stacked = dict(mix_norm_g=mix_norm_g, w_in=w_in, b_gate=b_gate, w_dw=w_dw, b_dw=b_dw, conv_ln_g=conv_ln_g,
                   conv_ln_b=conv_ln_b, w_conv_out=w_conv_out, q_norm_g=q_norm_g, w_uq=w_uq, kv_norm_g=kv_norm_g,
                   w_ukv=w_ukv, w_attn_out=w_attn_out, w_o=w_o, ffn_norm_g=ffn_norm_g, w_peer_q=w_peer_q,
                   peer_sub_keys=peer_sub_keys, peer_u=peer_u, peer_v=peer_v)
    depth = w_in.shape[0]
    half = QK_ROPE // 2
    inv_freq = ROPE_BASE ** (-jnp.arange(half, dtype=F32) / half)
    invf = jnp.concatenate([inv_freq, inv_freq, jnp.zeros((LANE - QK_ROPE,), F32)])[None, :]
    pos2 = positions.reshape(T, 1)
    h = x.reshape(T, D)
    for l in range(depth):
        p = {name: a[l] for name, a in stacked.items()}
        h1, pt = _layer(h, pos2, invf, p, B=B, S=S)
        last = l + 1 == depth
        h = _final(pt, h1, final_norm_g[None, :], tb=512, normalize=last)
    return h.reshape(B, S, D)
```

```python
import functools
import math

import jax
import jax.numpy as jnp
from jax import lax
from jax.experimental import pallas as pl
from jax.experimental.pallas import tpu as pltpu

F32 = jnp.float32
BF16 = jnp.bfloat16

EPS = 1e-6
CONV_WIDTH = 31
N_HEADS = 8
QK_NOPE = 128
QK_ROPE = 64
V_DIM = 128
Q_LORA = 256
KV_LORA = 128
ROPE_BASE = 10000.0
N_KEYS = 128
PEER_HEADS = 8
PEER_TOPK = 16
QK_PAD = 256
LANE = 128
SUB = 8
BF16_ROWS = 16
HALO = 32
VMEM_LIMIT = 56 * 1024 * 1024


def _cparams(sem):
    return pltpu.CompilerParams(dimension_semantics=sem, vmem_limit_bytes=VMEM_LIMIT)


def _dot(a, b):
    return jnp.dot(a, b, preferred_element_type=F32)


def _dot_nt(a, b):
    return lax.dot_general(a, b, (((1,), (1,)), ((), ())), preferred_element_type=F32)


def _sigmoid(x):
    return 1.0 / (1.0 + jnp.exp(-x))


def _rms(x, g):
    return x * lax.rsqrt(jnp.mean(x * x, axis=-1, keepdims=True) + EPS) * g


def _inproj_kernel(x_ref, g_ref, wv_ref, wg_ref, wq_ref, wkv_ref, wgl_ref, bg_ref,
                   hglu_ref, cq_ref, ckv_ref, gates_ref):
    a = _rms(x_ref[...], g_ref[...]).astype(BF16)
    val = _dot(a, wv_ref[...])
    gate = _dot(a, wg_ref[...])
    hglu_ref[...] = val * _sigmoid(gate)
    cq_ref[...] = _dot(a, wq_ref[...])
    ckv_ref[...] = _dot(a, wkv_ref[...])
    gates_ref[...] = _sigmoid(_dot(a, wgl_ref[...]) + bg_ref[...]).astype(BF16)


def _in_proj(x2, g, wv, wg, wq, wkv, wgl, bg, *, tm):
    T, D = x2.shape
    full = lambda a: pl.BlockSpec(a.shape, lambda i: (0, 0))
    row = lambda n: pl.BlockSpec((tm, n), lambda i: (i, 0))
    return pl.pallas_call(
        _inproj_kernel,
        grid=(T // tm,),
        in_specs=[row(D), full(g), full(wv), full(wg), full(wq), full(wkv), full(wgl), full(bg)],
        out_specs=[row(wv.shape[1]), row(wq.shape[1]), row(wkv.shape[1]), row(wgl.shape[1])],
        out_shape=[jax.ShapeDtypeStruct((T, wv.shape[1]), F32),
                   jax.ShapeDtypeStruct((T, wq.shape[1]), F32),
                   jax.ShapeDtypeStruct((T, wkv.shape[1]), F32),
                   jax.ShapeDtypeStruct((T, wgl.shape[1]), BF16)],
        compiler_params=_cparams(("parallel",)),
        name="in_proj",
    )(x2, g, wv, wg, wq, wkv, wgl, bg)


def _mla_proj_kernel(cq_ref, ckv_ref, pos_ref, invf_ref, qg_ref, wqa_ref, wqr_ref, kvg_ref, wk_ref, wv_ref,
                     q_ref, k_ref, v_ref, *, scale):
    qn = _rms(cq_ref[...], qg_ref[...]).astype(BF16)
    ckv_full = ckv_ref[...]
    kvn = _rms(ckv_full[:, :KV_LORA], kvg_ref[...]).astype(BF16)
    ang = pos_ref[...].astype(F32) * invf_ref[...]
    cosv = jnp.cos(ang)
    sinv = jnp.sin(ang)
    kr = (ckv_full[:, KV_LORA:KV_LORA + LANE] * cosv + ckv_full[:, KV_LORA + LANE:] * sinv).astype(BF16)
    qa = _dot(qn, wqa_ref[...])
    qr = _dot(qn, wqr_ref[...])
    kn = _dot(kvn, wk_ref[...])
    vt = _dot_nt(wv_ref[...], kvn)
    for h in range(N_HEADS):
        q_ref[0, h, :, 0:LANE] = (qa[:, h * QK_PAD:h * QK_PAD + LANE] * scale).astype(BF16)
        q_rope = qa[:, h * QK_PAD + LANE:(h + 1) * QK_PAD] * cosv + qr[:, h * LANE:(h + 1) * LANE] * sinv
        q_ref[0, h, :, LANE:QK_PAD] = (q_rope * scale).astype(BF16)
        k_ref[0, h, :, 0:LANE] = kn[:, h * LANE:(h + 1) * LANE].astype(BF16)
        k_ref[0, h, :, LANE:QK_PAD] = kr
        v_ref[0, h, :, :] = vt[h * V_DIM:(h + 1) * V_DIM, :].astype(BF16)


def _mla_proj(cq, ckv, pos2, invf, qg, wqa, wqr, kvg, wk, wv, *, B, S, tm):
    T = cq.shape[0]
    nb = S // tm
    full = lambda a: pl.BlockSpec(a.shape, lambda i: (0, 0))
    row = lambda n: pl.BlockSpec((tm, n), lambda i: (i, 0))
    head = lambda n: pl.BlockSpec((1, N_HEADS, tm, n), lambda i: (i // nb, 0, i % nb, 0))
    head_t = pl.BlockSpec((1, N_HEADS, V_DIM, tm), lambda i: (i // nb, 0, 0, i % nb))
    scale = math.log2(math.e) / math.sqrt(QK_NOPE + QK_ROPE)
    return pl.pallas_call(
        functools.partial(_mla_proj_kernel, scale=scale),
        grid=(T // tm,),
        in_specs=[row(cq.shape[1]), row(ckv.shape[1]), row(1), full(invf), full(qg), full(wqa), full(wqr),
                  full(kvg), full(wk), full(wv)],
        out_specs=[head(QK_PAD), head(QK_PAD), head_t],
        out_shape=[jax.ShapeDtypeStruct((B, N_HEADS, S, QK_PAD), BF16),
                   jax.ShapeDtypeStruct((B, N_HEADS, S, QK_PAD), BF16),
                   jax.ShapeDtypeStruct((B, N_HEADS, V_DIM, S), BF16)],
        compiler_params=_cparams(("parallel",)),
        name="mla_proj",
    )(cq, ckv, pos2, invf, qg, wqa, wqr, kvg, wk, wv)


def _flash_kernel(q_ref, k_ref, vt_ref, o_ref, acc_scr, *, bq, bk, heads):
    qi = pl.program_id(2)
    acc_scr[...] = jnp.zeros(acc_scr.shape, F32)

    def step(ki, carry, diagonal):
        off = pl.multiple_of(ki * bk, bk)
        out = []
        for hh in range(heads):
            m_prev, l_prev = carry[2 * hh], carry[2 * hh + 1]
            st = _dot_nt(k_ref[0, hh, pl.ds(off, bk), :], q_ref[0, hh])
            if diagonal:
                kpos = lax.broadcasted_iota(jnp.int32, (bk, bq), 0)
                qpos = lax.broadcasted_iota(jnp.int32, (bk, bq), 1)
                st = jnp.where(kpos <= qpos, st, -1e30)
            m_new = jnp.maximum(m_prev, jnp.max(st, axis=0, keepdims=True))
            p = jnp.exp2(st - m_new)
            alpha = jnp.exp2(m_prev - m_new)
            l_new = alpha * l_prev + jnp.sum(p, axis=0, keepdims=True)
            pv = _dot(vt_ref[0, hh, :, pl.ds(off, bk)], p.astype(BF16))
            acc_scr[hh] = alpha * acc_scr[hh] + pv
            out += [m_new, l_new]
        return tuple(out)

    init = (jnp.full((1, bq), -1e30, F32), jnp.zeros((1, bq), F32)) * heads
    pairs = qi // 2
    carry = lax.fori_loop(0, pairs, lambda kp, c: step(2 * kp + 1, step(2 * kp, c, False), False), init)
    carry = lax.fori_loop(2 * pairs, qi, lambda ki, c: step(ki, c, False), carry)
    carry = step(qi, carry, True)
    for hh in range(heads):
        o_ref[0, :, hh * V_DIM:(hh + 1) * V_DIM] = (acc_scr[hh] / carry[2 * hh + 1]).T.astype(BF16)


def _flash_attn(q, k, vt, *, bq, heads):
    B, H, S, _ = q.shape
    return pl.pallas_call(
        functools.partial(_flash_kernel, bq=bq, bk=bq, heads=heads),
        grid=(B, H // heads, S // bq),
        in_specs=[pl.BlockSpec((1, heads, bq, QK_PAD), lambda b, h, i: (b, h, i, 0)),
                  pl.BlockSpec((1, heads, S, QK_PAD), lambda b, h, i: (b, h, 0, 0)),
                  pl.BlockSpec((1, heads, V_DIM, S), lambda b, h, i: (b, h, 0, 0))],
        out_specs=pl.BlockSpec((1, bq, heads * V_DIM), lambda b, h, i: (b, i, h)),
        out_shape=jax.ShapeDtypeStruct((B, S, H * V_DIM), BF16),
        scratch_shapes=[pltpu.VMEM((heads, V_DIM, bq), F32)],
        compiler_params=_cparams(("parallel", "parallel", "arbitrary")),
        name="flash_attn",
    )(q, k, vt)


def _mix_kernel(hc_ref, hp_ref, o_ref, gates_ref, x_ref, wdw_ref, bdw_ref, lng_ref, lnb_ref,
                wco_ref, wao_ref, wo_ref, h1_ref, hbuf, cbuf, *, tm, blocks_per_seq, rows):
    i = pl.program_id(0)
    first = (i % blocks_per_seq) == 0
    hbuf[0, 0:HALO, :] = jnp.where(first, 0.0, hp_ref[...])
    hbuf[0, HALO:HALO + tm, :] = hc_ref[...]
    D = hc_ref.shape[1]
    span = HALO + tm - SUB
    for s in range(1, SUB):
        hbuf[s, 0:span, :] = hbuf[0, s:s + span, :]
    base = HALO - (CONV_WIDTH - 1)
    for c in range(D // LANE):
        cs = slice(c * LANE, (c + 1) * LANE)
        for r in range(tm // rows):
            acc = jnp.broadcast_to(bdw_ref[:, cs], (rows, LANE))
            for kk in range(CONV_WIDTH):
                shift = (base + kk) % SUB
                start = r * rows + base + kk - shift
                acc = acc + wdw_ref[kk:kk + 1, cs] * hbuf[shift, start:start + rows, cs]
            cbuf[r * rows:(r + 1) * rows, cs] = acc
    y = cbuf[...]
    mu = jnp.mean(y, axis=-1, keepdims=True)
    yc = y - mu
    var = jnp.mean(yc * yc, axis=-1, keepdims=True)
    yn = yc * lax.rsqrt(var + EPS) * lng_ref[...] + lnb_ref[...]
    act = (yn * _sigmoid(yn)).astype(BF16)
    y_conv = _dot(act, wco_ref[...])
    y_attn = _dot(o_ref[...], wao_ref[...])
    gts = gates_ref[...].astype(F32)
    mixed = (gts[:, :D] * y_conv + gts[:, D:] * y_attn).astype(BF16)
    h1_ref[...] = x_ref[...] + _dot(mixed, wo_ref[...])


def _mix(hglu, o2, gates, x2, wdw, bdw, lng, lnb, wco, wao, wo, *, S, tm):
    T, D = x2.shape
    full = lambda a: pl.BlockSpec(a.shape, lambda i: (0, 0))
    row = lambda n: pl.BlockSpec((tm, n), lambda i: (i, 0))
    per = tm // HALO
    halo = pl.BlockSpec((HALO, D), lambda i: (jnp.maximum(i * per - 1, 0), 0))
    return pl.pallas_call(
        functools.partial(_mix_kernel, tm=tm, blocks_per_seq=S // tm, rows=64),
        grid=(T // tm,),
        in_specs=[row(D), halo, row(D), row(2 * D), row(D), full(wdw), full(bdw), full(lng), full(lnb),
                  full(wco), full(wao), full(wo)],
        out_specs=row(D),
        out_shape=jax.ShapeDtypeStruct((T, D), F32),
        scratch_shapes=[pltpu.VMEM((SUB, HALO + tm, D), F32), pltpu.VMEM((tm, D), F32)],
        compiler_params=_cparams(("parallel",)),
        name="mix",
    )(hglu, hglu, o2, gates, x2, wdw, bdw, lng, lnb, wco, wao, wo)


def _peer_prep_kernel(h1_ref, g_ref, wpqt_ref, sk_ref, xnt_ref, st_ref):
    tm = h1_ref.shape[0]
    xn32 = _rms(h1_ref[...], g_ref[...])
    xnt_ref[...] = xn32.T.astype(BF16)
    xn = xn32.astype(BF16)
    qt = _dot_nt(wpqt_ref[...], xn).astype(BF16)
    groups = N_KEYS // SUB
    for hp in range(2 * PEER_HEADS):
        rs = slice(hp * N_KEYS, (hp + 1) * N_KEYS)
        s = _dot(sk_ref[hp], qt[rs, :])
        for a in range(tm // LANE):
            st_ref[hp * groups:(hp + 1) * groups, a * SUB:(a + 1) * SUB, :] = (
                s[:, a * LANE:(a + 1) * LANE].reshape(groups, SUB, LANE))


def _peer_prep(h1, g, wpqt, sk, *, tm):
    T, D = h1.shape
    G = wpqt.shape[0] // SUB
    return pl.pallas_call(
        _peer_prep_kernel,
        grid=(T // tm,),
        in_specs=[pl.BlockSpec((tm, D), lambda i: (i, 0)), pl.BlockSpec(g.shape, lambda i: (0, 0)),
                  pl.BlockSpec(wpqt.shape, lambda i: (0, 0)), pl.BlockSpec(sk.shape, lambda i: (0, 0, 0))],
        out_specs=[pl.BlockSpec((D, tm), lambda i: (0, i)),
                   pl.BlockSpec((G, tm // LANE * SUB, LANE), lambda i: (0, i, 0))],
        out_shape=[jax.ShapeDtypeStruct((D, T), BF16), jax.ShapeDtypeStruct((G, T // LANE * SUB, LANE), F32)],
        compiler_params=_cparams(("parallel",)),
        name="peer_prep",
    )(h1, g, wpqt, sk)


def _tree(op, xs):
    xs = list(xs)
    while len(xs) > 1:
        xs = [op(xs[i], xs[i + 1]) for i in range(0, len(xs) - 1, 2)] + ([xs[-1]] if len(xs) % 2 else [])
    return xs[0]


def _first_index(xs, m, chains):
    n = len(xs)
    per = n // chains
    heads = []
    for c in range(chains):
        idx = jnp.full(m.shape, float(n), F32)
        for i in reversed(range(c * per, (c + 1) * per)):
            idx = jnp.where(xs[i] == m, float(i), idx)
        heads.append(idx)
    return _tree(jnp.minimum, heads)


def _peer_topk_kernel(st_ref, r2_ref, lc_ref, e1_ref, e2_ref, work, rank2, vals, idx1):
    K = PEER_TOPK

    def rows(k):
        return k // SUB, pl.ds(k % SUB, SUB, stride=SUB)

    def score(p, k):
        g, r = rows(p * N_KEYS + k)
        return st_ref[g, r, :]

    for p in range(2):
        for k in range(N_KEYS):
            work[p, k] = score(p, k)
    for k in range(N_KEYS):
        rank2[k] = jnp.full((SUB, LANE), float(K), F32)

    def extract(a, carry):
        af = lax.convert_element_type(a, F32)
        for p in range(2):
            w = [work[p, k] for k in range(N_KEYS)]
            m = _tree(jnp.maximum, w)
            idx = _first_index(w, m, 8)
            vals[p, a] = m
            if p == 0:
                idx1[a] = idx
            for k in range(N_KEYS):
                sel = idx == float(k)
                work[p, k] = jnp.where(sel, -jnp.inf, w[k])
                if p == 1:
                    rank2[k] = jnp.where(sel, af, rank2[k])
        return carry

    lax.fori_loop(0, K, extract, 0)

    v1 = [vals[0, a] for a in range(K)]
    v2 = [vals[1, b] for b in range(K)]
    plen = [jnp.zeros((SUB, LANE), F32) for _ in range(K)]
    front = [v1[a] + v2[0] for a in range(K)]
    for _ in range(K):
        m = _tree(jnp.maximum, front)
        a_star = _first_index(front, m, 2)
        lsel = jnp.zeros((SUB, LANE), F32)
        v1sel = jnp.zeros((SUB, LANE), F32)
        sels = []
        for a in range(K):
            sel = a_star == float(a)
            sels.append(sel)
            plen[a] = plen[a] + jnp.where(sel, 1.0, 0.0)
            lsel = jnp.where(sel, plen[a], lsel)
            v1sel = jnp.where(sel, v1[a], v1sel)
        v2next = jnp.zeros((SUB, LANE), F32)
        for b in range(1, K):
            v2next = jnp.where(lsel == float(b), v2[b], v2next)
        fnew = jnp.where(lsel >= float(K), -jnp.inf, v1sel + v2next)
        for a in range(K):
            front[a] = jnp.where(sels[a], fnew, front[a])
    e1s = [jnp.exp(v1[a] - v1[0]) for a in range(K)]
    z = jnp.zeros((SUB, LANE), F32)
    for b in range(K):
        inner = jnp.zeros((SUB, LANE), F32)
        for a in range(K):
            inner = inner + jnp.where(plen[a] > float(b), e1s[a], 0.0)
        z = z + jnp.exp(v2[b] - v2[0]) * inner
    rz = 1.0 / z
    ids = [idx1[a] for a in range(K)]
    for k in range(N_KEYS):
        g, r = rows(k)
        lc = jnp.zeros((SUB, LANE), F32)
        for a in range(K):
            lc = jnp.where(ids[a] == float(k), plen[a], lc)
        lc_ref[0, g, r, :] = lc
        e1_ref[0, g, r, :] = jnp.exp(score(0, k) - v1[0]) * rz
        r2_ref[0, g, r, :] = rank2[k]
        e2_ref[0, g, r, :] = jnp.exp(score(1, k) - v2[0])


def _peer_topk(st, *, T, tokens):
    groups = N_KEYS // SUB
    rows_per_step = tokens // LANE * SUB
    tab = pl.BlockSpec((1, groups, rows_per_step, LANE), lambda t, h: (h, 0, t, 0))
    shp = jax.ShapeDtypeStruct((PEER_HEADS, groups, T // LANE * SUB, LANE), F32)
    vreg = (SUB, LANE)
    return pl.pallas_call(
        _peer_topk_kernel,
        grid=(T // tokens, PEER_HEADS),
        in_specs=[pl.BlockSpec((2 * groups, rows_per_step, LANE), lambda t, h: (h, t, 0))],
        out_specs=[tab, tab, tab, tab],
        out_shape=[shp, shp, shp, shp],
        scratch_shapes=[pltpu.VMEM((2, N_KEYS) + vreg, F32), pltpu.VMEM((N_KEYS,) + vreg, F32),
                        pltpu.VMEM((2, PEER_TOPK) + vreg, F32), pltpu.VMEM((PEER_TOPK,) + vreg, F32)],
        compiler_params=_cparams(("parallel", "parallel")),
        name="peer_topk",
    )(st)


def _gelu(a):
    return 0.5 * a * (1.0 + lax.erf(a * (1.0 / math.sqrt(2.0))))


def _interleave(major, minor):
    out, done = [], 0
    for n, item in enumerate(major):
        out.append(item)
        want = (n + 1) * len(minor) // len(major)
        out += minor[done:want]
        done = want
    return out


def _peer_dense_kernel(xnt_ref, u_ref, vt_ref, r2_ref, lc_ref, e1_ref, e2_ref, out_ref, r2b, e2b, w_scr, act_scr,
                       *, nchunks, nwork):
    s = pl.program_id(0)
    D, tb = xnt_ref.shape
    ec = u_ref.shape[0]
    tiles = ec // N_KEYS
    nl = tb // LANE
    packed = (N_KEYS // BF16_ROWS, BF16_ROWS, LANE)
    w_item = jnp.clip(s - 1, 0, nwork - 1)
    c_item = jnp.clip(s - 2, 0, nwork - 1)
    chunk = w_item % nchunks

    @pl.when(s == 0)
    def _():
        w_scr[...] = jnp.zeros(w_scr.shape, BF16)
        act_scr[...] = jnp.zeros(act_scr.shape, F32)

    @pl.when(chunk == 0)
    def _():
        for h in range(PEER_HEADS):
            for c in range(nl):
                r2b[h, c] = r2_ref[h, :, c * SUB:(c + 1) * SUB, :].reshape(packed).astype(BF16)
                e2b[h, c] = e2_ref[h, :, c * SUB:(c + 1) * SUB, :].reshape(packed).astype(BF16)

    @pl.when(c_item % nchunks == 0)
    def _():
        out_ref[...] = jnp.zeros(out_ref.shape, F32)

    cur = s % 2
    prev = 1 - cur
    zero = jnp.zeros((), BF16)
    halves = 2
    hw = tb // halves
    quarters = 4
    dp = D // quarters
    share = 1

    def scores(n):
        cs = slice(n * hw, (n + 1) * hw)
        act_scr[cur, :, cs] = _dot(u_ref[...], xnt_ref[:, cs])

    def combine(n):
        rs = slice(n * dp, (n + 1) * dp)
        out_ref[rs, :] += _dot(vt_ref[rs, :], w_scr[cur])

    def weights(i0, c):
        gs = [jnp.zeros(packed, BF16) for _ in range(share)]
        for h in range(PEER_HEADS):
            r2 = r2b[h, c]
            e2 = e2b[h, c]
            for n in range(share):
                row = c * SUB + i0 + n
                lc = jnp.broadcast_to(lc_ref[h, pl.ds(chunk, 1), row, :], (BF16_ROWS, LANE)).astype(BF16)
                e1 = jnp.broadcast_to(e1_ref[h, pl.ds(chunk, 1), row, :], (BF16_ROWS, LANE)).astype(BF16)
                gs[n] = gs[n] + jnp.where(r2 < lc[None], e2, zero) * e1[None]
        cs = slice(c * LANE, (c + 1) * LANE)
        for n in range(share):
            rs = slice((i0 + n) * N_KEYS, (i0 + n + 1) * N_KEYS)
            a = _gelu(act_scr[prev, rs, cs].astype(BF16))
            w_scr[prev, rs, cs] = gs[n].reshape(N_KEYS, LANE) * a

    mxu = [functools.partial(combine, 0), functools.partial(scores, 0), functools.partial(combine, 1),
           functools.partial(combine, 2), functools.partial(scores, 1), functools.partial(combine, 3)]
    vpu = [functools.partial(weights, i0, c) for i0 in range(0, tiles, share) for c in range(nl)]
    for stage in _interleave(mxu, vpu):
        stage()


def _peer_dense(xnt, u, vt, r2, lc, e1, e2, *, tb, ec):
    D, T = xnt.shape
    E = u.shape[0]
    assert ec == N_KEYS * SUB
    nchunks = E // ec
    nwork = (T // tb) * nchunks
    groups = N_KEYS // SUB
    nl = tb // LANE
    s_item = lambda s: jnp.minimum(s, nwork - 1)
    w_item = lambda s: jnp.clip(s - 1, 0, nwork - 1)
    c_item = lambda s: jnp.clip(s - 2, 0, nwork - 1)
    tab = pl.BlockSpec((PEER_HEADS, groups, nl * SUB, LANE), lambda s: (0, 0, w_item(s) // nchunks, 0))
    packed = (PEER_HEADS, nl, N_KEYS // BF16_ROWS, BF16_ROWS, LANE)
    return pl.pallas_call(
        functools.partial(_peer_dense_kernel, nchunks=nchunks, nwork=nwork),
        grid=(nwork + 2,),
        in_specs=[pl.BlockSpec((D, tb), lambda s: (0, s_item(s) // nchunks)),
                  pl.BlockSpec((ec, D), lambda s: (s_item(s) % nchunks, 0)),
                  pl.BlockSpec((D, ec), lambda s: (0, c_item(s) % nchunks)),
                  tab, tab, tab, tab],
        out_specs=pl.BlockSpec((D, tb), lambda s: (0, c_item(s) // nchunks)),
        out_shape=jax.ShapeDtypeStruct((D, T), F32),
        scratch_shapes=[pltpu.VMEM(packed, BF16), pltpu.VMEM(packed, BF16), pltpu.VMEM((2, ec, tb), BF16),
                        pltpu.VMEM((2, ec, tb), F32)],
        compiler_params=_cparams(("arbitrary",)),
        name="peer_dense",
    )(xnt, u, vt, r2, lc, e1, e2)


def _final_kernel(pt_ref, h1_ref, g_ref, out_ref, *, normalize):
    h2 = h1_ref[...] + pt_ref[...].T
    out_ref[...] = _rms(h2, g_ref[...]) if normalize else h2


def _final(pt, h1, g, *, tb, normalize):
    T, D = h1.shape
    return pl.pallas_call(
        functools.partial(_final_kernel, normalize=normalize),
        grid=(T // tb,),
        in_specs=[pl.BlockSpec((D, tb), lambda i: (0, i)), pl.BlockSpec((tb, D), lambda i: (i, 0)),
                  pl.BlockSpec(g.shape, lambda i: (0, 0))],
        out_specs=pl.BlockSpec((tb, D), lambda i: (i, 0)),
        out_shape=jax.ShapeDtypeStruct((T, D), F32),
        compiler_params=_cparams(("parallel",)),
        name="final_norm",
    )(pt, h1, g)


def _rot_cols(w):
    half = w.shape[-1] // 2
    return jnp.concatenate([-w[..., half:], w[..., :half]], axis=-1)


def _layer(h2d, pos2, invf, p, *, B, S):
    T, D = h2d.shape
    w_in = p["w_in"]
    o0, o1, o2, o3 = D, 2 * D, 2 * D + Q_LORA, 2 * D + Q_LORA + KV_LORA + QK_ROPE
    w_kr = w_in[:, o2 + KV_LORA:o3]
    zpad = jnp.zeros((D, LANE - QK_ROPE), F32)
    wkv = jnp.concatenate([w_in[:, o2:o2 + KV_LORA], w_kr, zpad, _rot_cols(w_kr), zpad], axis=1)
    hglu, cq, ckv, gates = _in_proj(
        h2d, p["mix_norm_g"][None, :], w_in[:, :o0].astype(BF16), w_in[:, o0:o1].astype(BF16),
        w_in[:, o1:o2].astype(BF16), wkv.astype(BF16), w_in[:, o3:].astype(BF16), p["b_gate"][None, :], tm=512)

    w_uq = p["w_uq"].reshape(Q_LORA, N_HEADS, QK_NOPE + QK_ROPE)
    zq = jnp.zeros((Q_LORA, N_HEADS, LANE - QK_ROPE), F32)
    wqa = jnp.concatenate([w_uq, zq], axis=-1).reshape(Q_LORA, N_HEADS * QK_PAD)
    wqr = jnp.concatenate([_rot_cols(w_uq[..., QK_NOPE:]), zq], axis=-1).reshape(Q_LORA, N_HEADS * LANE)
    w_ukv = p["w_ukv"].reshape(KV_LORA, N_HEADS, QK_NOPE + V_DIM)
    wk = w_ukv[..., :QK_NOPE].reshape(KV_LORA, N_HEADS * QK_NOPE)
    wv = w_ukv[..., QK_NOPE:].reshape(KV_LORA, N_HEADS * V_DIM)
    q, k, v = _mla_proj(cq, ckv, pos2, invf, p["q_norm_g"][None, :], wqa.astype(BF16), wqr.astype(BF16),
                        p["kv_norm_g"][None, :], wk.astype(BF16), wv.T.astype(BF16), B=B, S=S, tm=512)
    o = _flash_attn(q, k, v, bq=512, heads=4)

    h1 = _mix(hglu, o.reshape(T, D), gates, h2d, p["w_dw"].reshape(CONV_WIDTH, D), p["b_dw"][None, :],
              p["conv_ln_g"][None, :], p["conv_ln_b"][None, :], p["w_conv_out"].astype(BF16),
              p["w_attn_out"].astype(BF16), p["w_o"].astype(BF16), S=S, tm=256)

    sk = p["peer_sub_keys"].reshape(2 * PEER_HEADS, N_KEYS, -1).astype(BF16)
    xnt, st = _peer_prep(h1, p["ffn_norm_g"][None, :], p["w_peer_q"].T.astype(BF16), sk, tm=512)
    r2, lc, e1, e2 = _peer_topk(st, T=T, tokens=1024)
    pt = _peer_dense(xnt, p["peer_u"].astype(BF16), p["peer_v"].T.astype(BF16), r2, lc, e1, e2, tb=512, ec=1024)
    return h1, pt


def kernel(x, positions, mix_norm_g, w_in, b_gate, w_dw, b_dw, conv_ln_g, conv_ln_b, w_conv_out, q_norm_g, w_uq,
           kv_norm_g, w_ukv, w_attn_out, w_o, ffn_norm_g, w_peer_q, peer_sub_keys, peer_u, peer_v, final_norm_g):
    B, S, D = x.shape
    T = B * S
    stacked = dict(mix_norm_g=mix_norm_g, w_in=w_in, b_gate=b_gate, w_dw=w_dw, b_dw=b_dw, conv_ln_g=conv_ln_g,
                   conv_ln_b=conv_ln_b, w_conv_out=w_conv_out, q_norm_g=q_norm_g, w_uq=w_uq, kv_norm_g=kv_norm_g,
                   w_ukv=w_ukv, w_attn_out=w_attn_out, w_o=w_o, ffn_norm_g=ffn_norm_g, w_peer_q=w_peer_q,
                   peer_sub_keys=peer_sub_keys, peer_u=peer_u, peer_v=peer_v)
    depth = w_in.shape[0]
    half = QK_ROPE // 2
    inv_freq = ROPE_BASE ** (-jnp.arange(half, dtype=F32) / half)
    invf = jnp.concatenate([inv_freq, inv_freq, jnp.zeros((LANE - QK_ROPE,), F32)])[None, :]
    pos2 = positions.reshape(T, 1)
    h = x.reshape(T, D)
    for l in range(depth):
        p = {name: a[l] for name, a in stacked.items()}
        h1, pt = _layer(h, pos2, invf, p, B=B, S=S)
        last = l + 1 == depth
        h = _final(pt, h1, final_norm_g[None, :], tb=512, normalize=last)
    return h.reshape(B, S, D)
```

```python
import functools
import math
from typing import NamedTuple

import jax
import jax.numpy as jnp
from jax import lax
from jax.experimental import pallas as pl
from jax.experimental.pallas import tpu as pltpu

F32 = jnp.float32
BF16 = jnp.bfloat16

EPS = 1e-6
CONV_WIDTH = 31
N_HEADS = 8
QK_NOPE = 128
QK_ROPE = 64
V_DIM = 128
Q_LORA = 256
KV_LORA = 128
ROPE_BASE = 10000.0
N_KEYS = 128
PEER_HEADS = 8
PEER_TOPK = 16
QK_PAD = 256
LANE = 128
SUB = 8
BF16_ROWS = 16
HALO = 32
VMEM_LIMIT = 56 * 1024 * 1024


class _Tiles(NamedTuple):
    rows: int
    conv_rows: int
    attn_block: int
    attn_heads: int


TILES = _Tiles(rows=512, conv_rows=256, attn_block=512, attn_heads=4)


def _cparams(sem):
    return pltpu.CompilerParams(dimension_semantics=sem, vmem_limit_bytes=VMEM_LIMIT)


def _dot(a, b):
    return jnp.dot(a, b, preferred_element_type=F32)


def _dot_nt(a, b):
    return lax.dot_general(a, b, (((1,), (1,)), ((), ())), preferred_element_type=F32)


def _sigmoid(x):
    return 1.0 / (1.0 + jnp.exp(-x))


def _rms(x, g):
    return x * lax.rsqrt(jnp.mean(x * x, axis=-1, keepdims=True) + EPS) * g


def _inproj_kernel(x_ref, g_ref, wv_ref, wg_ref, wq_ref, wkv_ref, wgl_ref, bg_ref,
                   hglu_ref, cq_ref, ckv_ref, gates_ref):
    a = _rms(x_ref[...], g_ref[...]).astype(BF16)
    val = _dot(a, wv_ref[...])
    gate = _dot(a, wg_ref[...])
    hglu_ref[...] = val * _sigmoid(gate)
    cq_ref[...] = _dot(a, wq_ref[...])
    ckv_ref[...] = _dot(a, wkv_ref[...])
    gates_ref[...] = _sigmoid(_dot(a, wgl_ref[...]) + bg_ref[...]).astype(BF16)


def _in_proj(x2, g, wv, wg, wq, wkv, wgl, bg, *, tm):
    T, D = x2.shape
    full = lambda a: pl.BlockSpec(a.shape, lambda i: (0, 0))
    row = lambda n: pl.BlockSpec((tm, n), lambda i: (i, 0))
    return pl.pallas_call(
        _inproj_kernel,
        grid=(T // tm,),
        in_specs=[row(D), full(g), full(wv), full(wg), full(wq), full(wkv), full(wgl), full(bg)],
        out_specs=[row(wv.shape[1]), row(wq.shape[1]), row(wkv.shape[1]), row(wgl.shape[1])],
        out_shape=[jax.ShapeDtypeStruct((T, wv.shape[1]), F32),
                   jax.ShapeDtypeStruct((T, wq.shape[1]), F32),
                   jax.ShapeDtypeStruct((T, wkv.shape[1]), F32),
                   jax.ShapeDtypeStruct((T, wgl.shape[1]), BF16)],
        compiler_params=_cparams(("parallel",)),
        name="in_proj",
    )(x2, g, wv, wg, wq, wkv, wgl, bg)


def _mla_proj_kernel(cq_ref, ckv_ref, pos_ref, invf_ref, qg_ref, wqa_ref, wqr_ref, kvg_ref, wk_ref, wv_ref,
                     q_ref, k_ref, v_ref, *, scale):
    qn = _rms(cq_ref[...], qg_ref[...]).astype(BF16)
    ckv_full = ckv_ref[...]
    kvn = _rms(ckv_full[:, :KV_LORA], kvg_ref[...]).astype(BF16)
    ang = pos_ref[...].astype(F32) * invf_ref[...]
    cosv = jnp.cos(ang)
    sinv = jnp.sin(ang)
    kr = (ckv_full[:, KV_LORA:KV_LORA + LANE] * cosv + ckv_full[:, KV_LORA + LANE:] * sinv).astype(BF16)
    qa = _dot(qn, wqa_ref[...])
    qr = _dot(qn, wqr_ref[...])
    kn = _dot(kvn, wk_ref[...])
    vt = _dot_nt(wv_ref[...], kvn)
    for h in range(N_HEADS):
        q_ref[0, h, :, 0:LANE] = (qa[:, h * QK_PAD:h * QK_PAD + LANE] * scale).astype(BF16)
        q_rope = qa[:, h * QK_PAD + LANE:(h + 1) * QK_PAD] * cosv + qr[:, h * LANE:(h + 1) * LANE] * sinv
        q_ref[0, h, :, LANE:QK_PAD] = (q_rope * scale).astype(BF16)
        k_ref[0, h, :, 0:LANE] = kn[:, h * LANE:(h + 1) * LANE].astype(BF16)
        k_ref[0, h, :, LANE:QK_PAD] = kr
        v_ref[0, h, :, :] = vt[h * V_DIM:(h + 1) * V_DIM, :].astype(BF16)


def _mla_proj(cq, ckv, pos2, invf, qg, wqa, wqr, kvg, wk, wv, *, B, S, tm):
    T = cq.shape[0]
    nb = S // tm
    full = lambda a: pl.BlockSpec(a.shape, lambda i: (0, 0))
    row = lambda n: pl.BlockSpec((tm, n), lambda i: (i, 0))
    head = lambda n: pl.BlockSpec((1, N_HEADS, tm, n), lambda i: (i // nb, 0, i % nb, 0))
    head_t = pl.BlockSpec((1, N_HEADS, V_DIM, tm), lambda i: (i // nb, 0, 0, i % nb))
    scale = math.log2(math.e) / math.sqrt(QK_NOPE + QK_ROPE)
    return pl.pallas_call(
        functools.partial(_mla_proj_kernel, scale=scale),
        grid=(T // tm,),
        in_specs=[row(cq.shape[1]), row(ckv.shape[1]), row(1), full(invf), full(qg), full(wqa), full(wqr),
                  full(kvg), full(wk), full(wv)],
        out_specs=[head(QK_PAD), head(QK_PAD), head_t],
        out_shape=[jax.ShapeDtypeStruct((B, N_HEADS, S, QK_PAD), BF16),
                   jax.ShapeDtypeStruct((B, N_HEADS, S, QK_PAD), BF16),
                   jax.ShapeDtypeStruct((B, N_HEADS, V_DIM, S), BF16)],
        compiler_params=_cparams(("parallel",)),
        name="mla_proj",
    )(cq, ckv, pos2, invf, qg, wqa, wqr, kvg, wk, wv)


def _flash_kernel(q_ref, k_ref, vt_ref, o_ref, acc_scr, *, bq, bk, heads):
    qi = pl.program_id(2)
    acc_scr[...] = jnp.zeros(acc_scr.shape, F32)

    def step(ki, carry, diagonal):
        off = pl.multiple_of(ki * bk, bk)
        out = []
        for hh in range(heads):
            m_prev, l_prev = carry[2 * hh], carry[2 * hh + 1]
            st = _dot_nt(k_ref[0, hh, pl.ds(off, bk), :], q_ref[0, hh])
            if diagonal:
                kpos = lax.broadcasted_iota(jnp.int32, (bk, bq), 0)
                qpos = lax.broadcasted_iota(jnp.int32, (bk, bq), 1)
                st = jnp.where(kpos <= qpos, st, -1e30)
            m_new = jnp.maximum(m_prev, jnp.max(st, axis=0, keepdims=True))
            p = jnp.exp2(st - m_new)
            alpha = jnp.exp2(m_prev - m_new)
            l_new = alpha * l_prev + jnp.sum(p, axis=0, keepdims=True)
            pv = _dot(vt_ref[0, hh, :, pl.ds(off, bk)], p.astype(BF16))
            acc_scr[hh] = alpha * acc_scr[hh] + pv
            out += [m_new, l_new]
        return tuple(out)

    init = (jnp.full((1, bq), -1e30, F32), jnp.zeros((1, bq), F32)) * heads
    pairs = qi // 2
    carry = lax.fori_loop(0, pairs, lambda kp, c: step(2 * kp + 1, step(2 * kp, c, False), False), init)
    carry = lax.fori_loop(2 * pairs, qi, lambda ki, c: step(ki, c, False), carry)
    carry = step(qi, carry, True)
    for hh in range(heads):
        o_ref[0, :, hh * V_DIM:(hh + 1) * V_DIM] = (acc_scr[hh] / carry[2 * hh + 1]).T.astype(BF16)


def _flash_attn(q, k, vt, *, bq, heads):
    B, H, S, _ = q.shape
    return pl.pallas_call(
        functools.partial(_flash_kernel, bq=bq, bk=bq, heads=heads),
        grid=(B, H // heads, S // bq),
        in_specs=[pl.BlockSpec((1, heads, bq, QK_PAD), lambda b, h, i: (b, h, i, 0)),
                  pl.BlockSpec((1, heads, S, QK_PAD), lambda b, h, i: (b, h, 0, 0)),
                  pl.BlockSpec((1, heads, V_DIM, S), lambda b, h, i: (b, h, 0, 0))],
        out_specs=pl.BlockSpec((1, bq, heads * V_DIM), lambda b, h, i: (b, i, h)),
        out_shape=jax.ShapeDtypeStruct((B, S, H * V_DIM), BF16),
        scratch_shapes=[pltpu.VMEM((heads, V_DIM, bq), F32)],
        compiler_params=_cparams(("parallel", "parallel", "arbitrary")),
        name="flash_attn",
    )(q, k, vt)


def _mix_kernel(hc_ref, hp_ref, o_ref, gates_ref, x_ref, wdw_ref, bdw_ref, lng_ref, lnb_ref,
                wco_ref, wao_ref, wo_ref, h1_ref, hbuf, cbuf, *, tm, blocks_per_seq, rows):
    i = pl.program_id(0)
    first = (i % blocks_per_seq) == 0
    hbuf[0, 0:HALO, :] = jnp.where(first, 0.0, hp_ref[...])
    hbuf[0, HALO:HALO + tm, :] = hc_ref[...]
    D = hc_ref.shape[1]
    span = HALO + tm - SUB
    for s in range(1, SUB):
        hbuf[s, 0:span, :] = hbuf[0, s:s + span, :]
    base = HALO - (CONV_WIDTH - 1)
    for c in range(D // LANE):
        cs = slice(c * LANE, (c + 1) * LANE)
        for r in range(tm // rows):
            acc = jnp.broadcast_to(bdw_ref[:, cs], (rows, LANE))
            for kk in range(CONV_WIDTH):
                shift = (base + kk) % SUB
                start = r * rows + base + kk - shift
                acc = acc + wdw_ref[kk:kk + 1, cs] * hbuf[shift, start:start + rows, cs]
            cbuf[r * rows:(r + 1) * rows, cs] = acc
    y = cbuf[...]
    mu = jnp.mean(y, axis=-1, keepdims=True)
    yc = y - mu
    var = jnp.mean(yc * yc, axis=-1, keepdims=True)
    yn = yc * lax.rsqrt(var + EPS) * lng_ref[...] + lnb_ref[...]
    act = (yn * _sigmoid(yn)).astype(BF16)
    y_conv = _dot(act, wco_ref[...])
    y_attn = _dot(o_ref[...], wao_ref[...])
    gts = gates_ref[...].astype(F32)
    mixed = (gts[:, :D] * y_conv + gts[:, D:] * y_attn).astype(BF16)
    h1_ref[...] = x_ref[...] + _dot(mixed, wo_ref[...])


def _mix(hglu, o2, gates, x2, wdw, bdw, lng, lnb, wco, wao, wo, *, S, tm):
    T, D = x2.shape
    full = lambda a: pl.BlockSpec(a.shape, lambda i: (0, 0))
    row = lambda n: pl.BlockSpec((tm, n), lambda i: (i, 0))
    per = tm // HALO
    halo = pl.BlockSpec((HALO, D), lambda i: (jnp.maximum(i * per - 1, 0), 0))
    return pl.pallas_call(
        functools.partial(_mix_kernel, tm=tm, blocks_per_seq=S // tm, rows=64),
        grid=(T // tm,),
        in_specs=[row(D), halo, row(D), row(2 * D), row(D), full(wdw), full(bdw), full(lng), full(lnb),
                  full(wco), full(wao), full(wo)],
        out_specs=row(D),
        out_shape=jax.ShapeDtypeStruct((T, D), F32),
        scratch_shapes=[pltpu.VMEM((SUB, HALO + tm, D), F32), pltpu.VMEM((tm, D), F32)],
        compiler_params=_cparams(("parallel",)),
        name="mix",
    )(hglu, hglu, o2, gates, x2, wdw, bdw, lng, lnb, wco, wao, wo)


def _peer_prep_kernel(h1_ref, g_ref, wpqt_ref, sk_ref, xnt_ref, st_ref):
    tm = h1_ref.shape[0]
    xn32 = _rms(h1_ref[...], g_ref[...])
    xnt_ref[...] = xn32.T.astype(BF16)
    xn = xn32.astype(BF16)
    qt = _dot_nt(wpqt_ref[...], xn).astype(BF16)
    groups = N_KEYS // SUB
    for hp in range(2 * PEER_HEADS):
        rs = slice(hp * N_KEYS, (hp + 1) * N_KEYS)
        s = _dot(sk_ref[hp], qt[rs, :])
        for a in range(tm // LANE):
            st_ref[hp * groups:(hp + 1) * groups, a * SUB:(a + 1) * SUB, :] = (
                s[:, a * LANE:(a + 1) * LANE].reshape(groups, SUB, LANE))


def _peer_prep(h1, g, wpqt, sk, *, tm):
    T, D = h1.shape
    G = wpqt.shape[0] // SUB
    return pl.pallas_call(
        _peer_prep_kernel,
        grid=(T // tm,),
        in_specs=[pl.BlockSpec((tm, D), lambda i: (i, 0)), pl.BlockSpec(g.shape, lambda i: (0, 0)),
                  pl.BlockSpec(wpqt.shape, lambda i: (0, 0)), pl.BlockSpec(sk.shape, lambda i: (0, 0, 0))],
        out_specs=[pl.BlockSpec((D, tm), lambda i: (0, i)),
                   pl.BlockSpec((G, tm // LANE * SUB, LANE), lambda i: (0, i, 0))],
        out_shape=[jax.ShapeDtypeStruct((D, T), BF16), jax.ShapeDtypeStruct((G, T // LANE * SUB, LANE), F32)],
        compiler_params=_cparams(("parallel",)),
        name="peer_prep",
    )(h1, g, wpqt, sk)


def _tree(op, xs):
    xs = list(xs)
    while len(xs) > 1:
        xs = [op(xs[i], xs[i + 1]) for i in range(0, len(xs) - 1, 2)] + ([xs[-1]] if len(xs) % 2 else [])
    return xs[0]


def _first_index(xs, m, chains):
    n = len(xs)
    per = n // chains
    heads = []
    for c in range(chains):
        idx = jnp.full(m.shape, float(n), F32)
        for i in reversed(range(c * per, (c + 1) * per)):
            idx = jnp.where(xs[i] == m, float(i), idx)
        heads.append(idx)
    return _tree(jnp.minimum, heads)


def _peer_topk_kernel(st_ref, r2_ref, lc_ref, e1_ref, e2_ref, work, rank2, vals, idx1):
    K = PEER_TOPK

    def rows(k):
        return k // SUB, pl.ds(k % SUB, SUB, stride=SUB)

    def score(p, k):
        g, r = rows(p * N_KEYS + k)
        return st_ref[g, r, :]

    for p in range(2):
        for k in range(N_KEYS):
            work[p, k] = score(p, k)
    for k in range(N_KEYS):
        rank2[k] = jnp.full((SUB, LANE), float(K), F32)

    def extract(a, carry):
        af = lax.convert_element_type(a, F32)
        for p in range(2):
            w = [work[p, k] for k in range(N_KEYS)]
            m = _tree(jnp.maximum, w)
            idx = _first_index(w, m, 8)
            vals[p, a] = m
            if p == 0:
                idx1[a] = idx
            for k in range(N_KEYS):
                sel = idx == float(k)
                work[p, k] = jnp.where(sel, -jnp.inf, w[k])
                if p == 1:
                    rank2[k] = jnp.where(sel, af, rank2[k])
        return carry

    lax.fori_loop(0, K, extract, 0)

    v1 = [vals[0, a] for a in range(K)]
    v2 = [vals[1, b] for b in range(K)]
    plen = [jnp.zeros((SUB, LANE), F32) for _ in range(K)]
    front = [v1[a] + v2[0] for a in range(K)]
    for _ in range(K):
        m = _tree(jnp.maximum, front)
        a_star = _first_index(front, m, 2)
        lsel = jnp.zeros((SUB, LANE), F32)
        v1sel = jnp.zeros((SUB, LANE), F32)
        sels = []
        for a in range(K):
            sel = a_star == float(a)
            sels.append(sel)
            plen[a] = plen[a] + jnp.where(sel, 1.0, 0.0)
            lsel = jnp.where(sel, plen[a], lsel)
            v1sel = jnp.where(sel, v1[a], v1sel)
        v2next = jnp.zeros((SUB, LANE), F32)
        for b in range(1, K):
            v2next = jnp.where(lsel == float(b), v2[b], v2next)
        fnew = jnp.where(lsel >= float(K), -jnp.inf, v1sel + v2next)
        for a in range(K):
            front[a] = jnp.where(sels[a], fnew, front[a])
    e1s = [jnp.exp(v1[a] - v1[0]) for a in range(K)]
    z = jnp.zeros((SUB, LANE), F32)
    for b in range(K):
        inner = jnp.zeros((SUB, LANE), F32)
        for a in range(K):
            inner = inner + jnp.where(plen[a] > float(b), e1s[a], 0.0)
        z = z + jnp.exp(v2[b] - v2[0]) * inner
    rz = 1.0 / z
    ids = [idx1[a] for a in range(K)]
    for k in range(N_KEYS):
        g, r = rows(k)
        lc = jnp.zeros((SUB, LANE), F32)
        for a in range(K):
            lc = jnp.where(ids[a] == float(k), plen[a], lc)
        lc_ref[0, g, r, :] = lc
        e1_ref[0, g, r, :] = jnp.exp(score(0, k) - v1[0]) * rz
        r2_ref[0, g, r, :] = rank2[k]
        e2_ref[0, g, r, :] = jnp.exp(score(1, k) - v2[0])


def _peer_topk(st, *, T, tokens):
    groups = N_KEYS // SUB
    rows_per_step = tokens // LANE * SUB
    tab = pl.BlockSpec((1, groups, rows_per_step, LANE), lambda t, h: (h, 0, t, 0))
    shp = jax.ShapeDtypeStruct((PEER_HEADS, groups, T // LANE * SUB, LANE), F32)
    vreg = (SUB, LANE)
    return pl.pallas_call(
        _peer_topk_kernel,
        grid=(T // tokens, PEER_HEADS),
        in_specs=[pl.BlockSpec((2 * groups, rows_per_step, LANE), lambda t, h: (h, t, 0))],
        out_specs=[tab, tab, tab, tab],
        out_shape=[shp, shp, shp, shp],
        scratch_shapes=[pltpu.VMEM((2, N_KEYS) + vreg, F32), pltpu.VMEM((N_KEYS,) + vreg, F32),
                        pltpu.VMEM((2, PEER_TOPK) + vreg, F32), pltpu.VMEM((PEER_TOPK,) + vreg, F32)],
        compiler_params=_cparams(("parallel", "parallel")),
        name="peer_topk",
    )(st)


def _gelu(a):
    return 0.5 * a * (1.0 + lax.erf(a * (1.0 / math.sqrt(2.0))))


def _interleave(major, minor):
    out, done = [], 0
    for n, item in enumerate(major):
        out.append(item)
        want = (n + 1) * len(minor) // len(major)
        out += minor[done:want]
        done = want
    return out


def _peer_dense_kernel(xnt_ref, u_ref, vt_ref, r2_ref, lc_ref, e1_ref, e2_ref, out_ref, r2b, e2b, w_scr, act_scr,
                       *, nchunks, nwork):
    s = pl.program_id(0)
    D, tb = xnt_ref.shape
    ec = u_ref.shape[0]
    tiles = ec // N_KEYS
    nl = tb // LANE
    packed = (N_KEYS // BF16_ROWS, BF16_ROWS, LANE)
    w_item = jnp.clip(s - 1, 0, nwork - 1)
    c_item = jnp.clip(s - 2, 0, nwork - 1)
    chunk = w_item % nchunks

    @pl.when(s == 0)
    def _():
        w_scr[...] = jnp.zeros(w_scr.shape, BF16)
        act_scr[...] = jnp.zeros(act_scr.shape, F32)

    @pl.when(chunk == 0)
    def _():
        for h in range(PEER_HEADS):
            for c in range(nl):
                r2b[h, c] = r2_ref[h, :, c * SUB:(c + 1) * SUB, :].reshape(packed).astype(BF16)
                e2b[h, c] = e2_ref[h, :, c * SUB:(c + 1) * SUB, :].reshape(packed).astype(BF16)

    @pl.when(c_item % nchunks == 0)
    def _():
        out_ref[...] = jnp.zeros(out_ref.shape, F32)

    cur = s % 2
    prev = 1 - cur
    zero = jnp.zeros((), BF16)
    halves = 2
    hw = tb // halves
    quarters = 4
    dp = D // quarters
    share = 2

    def scores(n):
        cs = slice(n * hw, (n + 1) * hw)
        act_scr[cur, :, cs] = _dot(u_ref[...], xnt_ref[:, cs])

    def combine(n):
        rs = slice(n * dp, (n + 1) * dp)
        out_ref[rs, :] += _dot(vt_ref[rs, :], w_scr[cur])

    def weights(i0, c):
        gs = [jnp.zeros(packed, BF16) for _ in range(share)]
        for h in range(PEER_HEADS):
            r2 = r2b[h, c]
            e2 = e2b[h, c]
            for n in range(share):
                row = c * SUB + i0 + n
                lc = jnp.broadcast_to(lc_ref[h, pl.ds(chunk, 1), row, :], (BF16_ROWS, LANE)).astype(BF16)
                e1 = jnp.broadcast_to(e1_ref[h, pl.ds(chunk, 1), row, :], (BF16_ROWS, LANE)).astype(BF16)
                gs[n] = gs[n] + jnp.where(r2 < lc[None], e2, zero) * e1[None]
        cs = slice(c * LANE, (c + 1) * LANE)
        for n in range(share):
            rs = slice((i0 + n) * N_KEYS, (i0 + n + 1) * N_KEYS)
            a = _gelu(act_scr[prev, rs, cs].astype(BF16))
            w_scr[prev, rs, cs] = gs[n].reshape(N_KEYS, LANE) * a

    mxu = [functools.partial(combine, 0), functools.partial(scores, 0), functools.partial(combine, 1),
           functools.partial(combine, 2), functools.partial(scores, 1), functools.partial(combine, 3)]
    vpu = [functools.partial(weights, i0, c) for i0 in range(0, tiles, share) for c in range(nl)]
    for stage in _interleave(mxu, vpu):
        stage()


def _peer_dense(xnt, u, vt, r2, lc, e1, e2, *, tb, ec):
    D, T = xnt.shape
    E = u.shape[0]
    assert ec == N_KEYS * SUB
    nchunks = E // ec
    nwork = (T // tb) * nchunks
    groups = N_KEYS // SUB
    nl = tb // LANE
    s_item = lambda s: jnp.minimum(s, nwork - 1)
    w_item = lambda s: jnp.clip(s - 1, 0, nwork - 1)
    c_item = lambda s: jnp.clip(s - 2, 0, nwork - 1)
    tab = pl.BlockSpec((PEER_HEADS, groups, nl * SUB, LANE), lambda s: (0, 0, w_item(s) // nchunks, 0))
    packed = (PEER_HEADS, nl, N_KEYS // BF16_ROWS, BF16_ROWS, LANE)
    return pl.pallas_call(
        functools.partial(_peer_dense_kernel, nchunks=nchunks, nwork=nwork),
        grid=(nwork + 2,),
        in_specs=[pl.BlockSpec((D, tb), lambda s: (0, s_item(s) // nchunks)),
                  pl.BlockSpec((ec, D), lambda s: (s_item(s) % nchunks, 0)),
                  pl.BlockSpec((D, ec), lambda s: (0, c_item(s) % nchunks)),
                  tab, tab, tab, tab],
        out_specs=pl.BlockSpec((D, tb), lambda s: (0, c_item(s) // nchunks)),
        out_shape=jax.ShapeDtypeStruct((D, T), F32),
        scratch_shapes=[pltpu.VMEM(packed, BF16), pltpu.VMEM(packed, BF16), pltpu.VMEM((2, ec, tb), BF16),
                        pltpu.VMEM((2, ec, tb), F32)],
        compiler_params=_cparams(("arbitrary",)),
        name="peer_dense",
    )(xnt, u, vt, r2, lc, e1, e2)


def _final_kernel(pt_ref, h1_ref, g_ref, out_ref, *, normalize):
    h2 = h1_ref[...] + pt_ref[...].T
    out_ref[...] = _rms(h2, g_ref[...]) if normalize else h2


def _final(pt, h1, g, *, tb, normalize):
    T, D = h1.shape
    return pl.pallas_call(
        functools.partial(_final_kernel, normalize=normalize),
        grid=(T // tb,),
        in_specs=[pl.BlockSpec((D, tb), lambda i: (0, i)), pl.BlockSpec((tb, D), lambda i: (i, 0)),
                  pl.BlockSpec(g.shape, lambda i: (0, 0))],
        out_specs=pl.BlockSpec((tb, D), lambda i: (i, 0)),
        out_shape=jax.ShapeDtypeStruct((T, D), F32),
        compiler_params=_cparams(("parallel",)),
        name="final_norm",
    )(pt, h1, g)


def _rot_cols(w):
    half = w.shape[-1] // 2
    return jnp.concatenate([-w[..., half:], w[..., :half]], axis=-1)


def _layer(h2d, pos2, invf, p, *, B, S):
    T, D = h2d.shape
    w_in = p["w_in"]
    o0, o1, o2, o3 = D, 2 * D, 2 * D + Q_LORA, 2 * D + Q_LORA + KV_LORA + QK_ROPE
    w_kr = w_in[:, o2 + KV_LORA:o3]
    zpad = jnp.zeros((D, LANE - QK_ROPE), F32)
    wkv = jnp.concatenate([w_in[:, o2:o2 + KV_LORA], w_kr, zpad, _rot_cols(w_kr), zpad], axis=1)
    hglu, cq, ckv, gates = _in_proj(
        h2d, p["mix_norm_g"][None, :], w_in[:, :o0].astype(BF16), w_in[:, o0:o1].astype(BF16),
        w_in[:, o1:o2].astype(BF16), wkv.astype(BF16), w_in[:, o3:].astype(BF16), p["b_gate"][None, :],
        tm=TILES.rows)

    w_uq = p["w_uq"].reshape(Q_LORA, N_HEADS, QK_NOPE + QK_ROPE)
    zq = jnp.zeros((Q_LORA, N_HEADS, LANE - QK_ROPE), F32)
    wqa = jnp.concatenate([w_uq, zq], axis=-1).reshape(Q_LORA, N_HEADS * QK_PAD)
    wqr = jnp.concatenate([_rot_cols(w_uq[..., QK_NOPE:]), zq], axis=-1).reshape(Q_LORA, N_HEADS * LANE)
    w_ukv = p["w_ukv"].reshape(KV_LORA, N_HEADS, QK_NOPE + V_DIM)
    wk = w_ukv[..., :QK_NOPE].reshape(KV_LORA, N_HEADS * QK_NOPE)
    wv = w_ukv[..., QK_NOPE:].reshape(KV_LORA, N_HEADS * V_DIM)
    q, k, v = _mla_proj(cq, ckv, pos2, invf, p["q_norm_g"][None, :], wqa.astype(BF16), wqr.astype(BF16),
                        p["kv_norm_g"][None, :], wk.astype(BF16), wv.T.astype(BF16), B=B, S=S, tm=TILES.rows)
    o = _flash_attn(q, k, v, bq=TILES.attn_block, heads=TILES.attn_heads)

    h1 = _mix(hglu, o.reshape(T, D), gates, h2d, p["w_dw"].reshape(CONV_WIDTH, D), p["b_dw"][None, :],
              p["conv_ln_g"][None, :], p["conv_ln_b"][None, :], p["w_conv_out"].astype(BF16),
              p["w_attn_out"].astype(BF16), p["w_o"].astype(BF16), S=S, tm=TILES.conv_rows)

    sk = p["peer_sub_keys"].reshape(2 * PEER_HEADS, N_KEYS, -1).astype(BF16)
    xnt, st = _peer_prep(h1, p["ffn_norm_g"][None, :], p["w_peer_q"].T.astype(BF16), sk, tm=TILES.rows)
    r2, lc, e1, e2 = _peer_topk(st, T=T, tokens=SUB * LANE)
    pt = _peer_dense(xnt, p["peer_u"].astype(BF16), p["peer_v"].T.astype(BF16), r2, lc, e1, e2,
                     tb=TILES.rows, ec=N_KEYS * SUB)
    return h1, pt


def kernel(x, positions, mix_norm_g, w_in, b_gate, w_dw, b_dw, conv_ln_g, conv_ln_b, w_conv_out, q_norm_g, w_uq,
           kv_norm_g, w_ukv, w_attn_out, w_o, ffn_norm_g, w_peer_q, peer_sub_keys, peer_u, peer_v, final_norm_g):
    B, S, D = x.shape
    T = B * S
    stacked = dict(mix_norm_g=mix_norm_g, w_in=w_in, b_gate=b_gate, w_dw=w_dw, b_dw=b_dw, conv_ln_g=conv_ln_g,
                   conv_ln_b=conv_ln_b, w_conv_out=w_conv_out, q_norm_g=q_norm_g, w_uq=w_uq, kv_norm_g=kv_norm_g,
                   w_ukv=w_ukv, w_attn_out=w_attn_out, w_o=w_o, ffn_norm_g=ffn_norm_g, w_peer_q=w_peer_q,
                   peer_sub_keys=peer_sub_keys, peer_u=peer_u, peer_v=peer_v)
    depth = w_in.shape[0]
    half = QK_ROPE // 2
    inv_freq = ROPE_BASE ** (-jnp.arange(half, dtype=F32) / half)
    invf = jnp.concatenate([inv_freq, inv_freq, jnp.zeros((LANE - QK_ROPE,), F32)])[None, :]
    pos2 = positions.reshape(T, 1)
    h = x.reshape(T, D)
    for l in range(depth):
        p = {name: a[l] for name, a in stacked.items()}
        h1, pt = _layer(h, pos2, invf, p, B=B, S=S)
        last = l + 1 == depth
        h = _final(pt, h1, final_norm_g[None, :], tb=TILES.rows, normalize=last)
    return h.reshape(B, S, D)
```

```python
import functools
import math
from typing import NamedTuple

import jax
import jax.numpy as jnp
from jax import lax
from jax.experimental import pallas as pl
from jax.experimental.pallas import tpu as pltpu

F32 = jnp.float32
BF16 = jnp.bfloat16

EPS = 1e-6
CONV_WIDTH = 31
N_HEADS = 8
QK_NOPE = 128
QK_ROPE = 64
V_DIM = 128
Q_LORA = 256
KV_LORA = 128
ROPE_BASE = 10000.0
N_KEYS = 128
PEER_HEADS = 8
PEER_TOPK = 16
QK_PAD = 256
LANE = 128
SUB = 8
BF16_ROWS = 16
HALO = 32
VMEM_LIMIT = 56 * 1024 * 1024


class _Tiles(NamedTuple):
    rows: int
    conv_rows: int
    attn_block: int
    attn_heads: int


TILES = _Tiles(rows=512, conv_rows=256, attn_block=512, attn_heads=4)


def _cparams(sem):
    return pltpu.CompilerParams(dimension_semantics=sem, vmem_limit_bytes=VMEM_LIMIT)


def _dot(a, b):
    return jnp.dot(a, b, preferred_element_type=F32)


def _dot_nt(a, b):
    return lax.dot_general(a, b, (((1,), (1,)), ((), ())), preferred_element_type=F32)


def _sigmoid(x):
    return 1.0 / (1.0 + jnp.exp(-x))


def _rms(x, g):
    return x * lax.rsqrt(jnp.mean(x * x, axis=-1, keepdims=True) + EPS) * g


def _inproj_kernel(x_ref, g_ref, wv_ref, wg_ref, wq_ref, wkv_ref, wgl_ref, bg_ref,
                   hglu_ref, cq_ref, ckv_ref, gates_ref):
    a = _rms(x_ref[...], g_ref[...]).astype(BF16)
    val = _dot(a, wv_ref[...])
    gate = _dot(a, wg_ref[...])
    hglu_ref[...] = val * _sigmoid(gate)
    cq_ref[...] = _dot(a, wq_ref[...])
    ckv_ref[...] = _dot(a, wkv_ref[...])
    gates_ref[...] = _sigmoid(_dot(a, wgl_ref[...]) + bg_ref[...]).astype(BF16)


def _in_proj(x2, g, wv, wg, wq, wkv, wgl, bg, *, tm):
    T, D = x2.shape
    full = lambda a: pl.BlockSpec(a.shape, lambda i: (0, 0))
    row = lambda n: pl.BlockSpec((tm, n), lambda i: (i, 0))
    return pl.pallas_call(
        _inproj_kernel,
        grid=(T // tm,),
        in_specs=[row(D), full(g), full(wv), full(wg), full(wq), full(wkv), full(wgl), full(bg)],
        out_specs=[row(wv.shape[1]), row(wq.shape[1]), row(wkv.shape[1]), row(wgl.shape[1])],
        out_shape=[jax.ShapeDtypeStruct((T, wv.shape[1]), F32),
                   jax.ShapeDtypeStruct((T, wq.shape[1]), F32),
                   jax.ShapeDtypeStruct((T, wkv.shape[1]), F32),
                   jax.ShapeDtypeStruct((T, wgl.shape[1]), BF16)],
        compiler_params=_cparams(("parallel",)),
        name="in_proj",
    )(x2, g, wv, wg, wq, wkv, wgl, bg)


def _mla_proj_kernel(cq_ref, ckv_ref, pos_ref, invf_ref, qg_ref, wqa_ref, wqr_ref, kvg_ref, wk_ref, wv_ref,
                     q_ref, k_ref, v_ref, *, scale):
    qn = _rms(cq_ref[...], qg_ref[...]).astype(BF16)
    ckv_full = ckv_ref[...]
    kvn = _rms(ckv_full[:, :KV_LORA], kvg_ref[...]).astype(BF16)
    ang = pos_ref[...].astype(F32) * invf_ref[...]
    cosv = jnp.cos(ang)
    sinv = jnp.sin(ang)
    kr = (ckv_full[:, KV_LORA:KV_LORA + LANE] * cosv + ckv_full[:, KV_LORA + LANE:] * sinv).astype(BF16)
    qa = _dot(qn, wqa_ref[...])
    qr = _dot(qn, wqr_ref[...])
    kn = _dot(kvn, wk_ref[...])
    vt = _dot_nt(wv_ref[...], kvn)
    for h in range(N_HEADS):
        q_ref[0, h, :, 0:LANE] = (qa[:, h * QK_PAD:h * QK_PAD + LANE] * scale).astype(BF16)
        q_rope = qa[:, h * QK_PAD + LANE:(h + 1) * QK_PAD] * cosv + qr[:, h * LANE:(h + 1) * LANE] * sinv
        q_ref[0, h, :, LANE:QK_PAD] = (q_rope * scale).astype(BF16)
        k_ref[0, h, :, 0:LANE] = kn[:, h * LANE:(h + 1) * LANE].astype(BF16)
        k_ref[0, h, :, LANE:QK_PAD] = kr
        v_ref[0, h, :, :] = vt[h * V_DIM:(h + 1) * V_DIM, :].astype(BF16)


def _mla_proj(cq, ckv, pos2, invf, qg, wqa, wqr, kvg, wk, wv, *, B, S, tm):
    T = cq.shape[0]
    nb = S // tm
    full = lambda a: pl.BlockSpec(a.shape, lambda i: (0, 0))
    row = lambda n: pl.BlockSpec((tm, n), lambda i: (i, 0))
    head = lambda n: pl.BlockSpec((1, N_HEADS, tm, n), lambda i: (i // nb, 0, i % nb, 0))
    head_t = pl.BlockSpec((1, N_HEADS, V_DIM, tm), lambda i: (i // nb, 0, 0, i % nb))
    scale = math.log2(math.e) / math.sqrt(QK_NOPE + QK_ROPE)
    return pl.pallas_call(
        functools.partial(_mla_proj_kernel, scale=scale),
        grid=(T // tm,),
        in_specs=[row(cq.shape[1]), row(ckv.shape[1]), row(1), full(invf), full(qg), full(wqa), full(wqr),
                  full(kvg), full(wk), full(wv)],
        out_specs=[head(QK_PAD), head(QK_PAD), head_t],
        out_shape=[jax.ShapeDtypeStruct((B, N_HEADS, S, QK_PAD), BF16),
                   jax.ShapeDtypeStruct((B, N_HEADS, S, QK_PAD), BF16),
                   jax.ShapeDtypeStruct((B, N_HEADS, V_DIM, S), BF16)],
        compiler_params=_cparams(("parallel",)),
        name="mla_proj",
    )(cq, ckv, pos2, invf, qg, wqa, wqr, kvg, wk, wv)


def _flash_kernel(q_ref, k_ref, vt_ref, o_ref, acc_scr, *, bq, bk, heads):
    qi = pl.program_id(2)
    acc_scr[...] = jnp.zeros(acc_scr.shape, F32)

    def step(ki, carry, diagonal):
        off = pl.multiple_of(ki * bk, bk)
        out = []
        for hh in range(heads):
            m_prev, l_prev = carry[2 * hh], carry[2 * hh + 1]
            st = _dot_nt(k_ref[0, hh, pl.ds(off, bk), :], q_ref[0, hh])
            if diagonal:
                kpos = lax.broadcasted_iota(jnp.int32, (bk, bq), 0)
                qpos = lax.broadcasted_iota(jnp.int32, (bk, bq), 1)
                st = jnp.where(kpos <= qpos, st, -1e30)
            m_new = jnp.maximum(m_prev, jnp.max(st, axis=0, keepdims=True))
            p = jnp.exp2(st - m_new)
            alpha = jnp.exp2(m_prev - m_new)
            l_new = alpha * l_prev + jnp.sum(p, axis=0, keepdims=True)
            pv = _dot(vt_ref[0, hh, :, pl.ds(off, bk)], p.astype(BF16))
            acc_scr[hh] = alpha * acc_scr[hh] + pv
            out += [m_new, l_new]
        return tuple(out)

    init = (jnp.full((1, bq), -1e30, F32), jnp.zeros((1, bq), F32)) * heads
    pairs = qi // 2
    carry = lax.fori_loop(0, pairs, lambda kp, c: step(2 * kp + 1, step(2 * kp, c, False), False), init)
    carry = lax.fori_loop(2 * pairs, qi, lambda ki, c: step(ki, c, False), carry)
    carry = step(qi, carry, True)
    for hh in range(heads):
        o_ref[0, :, hh * V_DIM:(hh + 1) * V_DIM] = (acc_scr[hh] / carry[2 * hh + 1]).T.astype(BF16)


def _flash_attn(q, k, vt, *, bq, heads):
    B, H, S, _ = q.shape
    return pl.pallas_call(
        functools.partial(_flash_kernel, bq=bq, bk=bq, heads=heads),
        grid=(B, H // heads, S // bq),
        in_specs=[pl.BlockSpec((1, heads, bq, QK_PAD), lambda b, h, i: (b, h, i, 0)),
                  pl.BlockSpec((1, heads, S, QK_PAD), lambda b, h, i: (b, h, 0, 0)),
                  pl.BlockSpec((1, heads, V_DIM, S), lambda b, h, i: (b, h, 0, 0))],
        out_specs=pl.BlockSpec((1, bq, heads * V_DIM), lambda b, h, i: (b, i, h)),
        out_shape=jax.ShapeDtypeStruct((B, S, H * V_DIM), BF16),
        scratch_shapes=[pltpu.VMEM((heads, V_DIM, bq), F32)],
        compiler_params=_cparams(("parallel", "parallel", "arbitrary")),
        name="flash_attn",
    )(q, k, vt)


def _mix_kernel(hc_ref, hp_ref, o_ref, gates_ref, x_ref, wdw_ref, bdw_ref, lng_ref, lnb_ref,
                wco_ref, wao_ref, wo_ref, h1_ref, hbuf, cbuf, *, tm, blocks_per_seq, rows):
    i = pl.program_id(0)
    first = (i % blocks_per_seq) == 0
    hbuf[0, 0:HALO, :] = jnp.where(first, 0.0, hp_ref[...])
    hbuf[0, HALO:HALO + tm, :] = hc_ref[...]
    D = hc_ref.shape[1]
    span = HALO + tm - SUB
    for s in range(1, SUB):
        hbuf[s, 0:span, :] = hbuf[0, s:s + span, :]
    base = HALO - (CONV_WIDTH - 1)
    for c in range(D // LANE):
        cs = slice(c * LANE, (c + 1) * LANE)
        for r in range(tm // rows):
            acc = jnp.broadcast_to(bdw_ref[:, cs], (rows, LANE))
            for kk in range(CONV_WIDTH):
                shift = (base + kk) % SUB
                start = r * rows + base + kk - shift
                acc = acc + wdw_ref[kk:kk + 1, cs] * hbuf[shift, start:start + rows, cs]
            cbuf[r * rows:(r + 1) * rows, cs] = acc
    y = cbuf[...]
    mu = jnp.mean(y, axis=-1, keepdims=True)
    yc = y - mu
    var = jnp.mean(yc * yc, axis=-1, keepdims=True)
    yn = yc * lax.rsqrt(var + EPS) * lng_ref[...] + lnb_ref[...]
    act = (yn * _sigmoid(yn)).astype(BF16)
    y_conv = _dot(act, wco_ref[...])
    y_attn = _dot(o_ref[...], wao_ref[...])
    gts = gates_ref[...].astype(F32)
    mixed = (gts[:, :D] * y_conv + gts[:, D:] * y_attn).astype(BF16)
    h1_ref[...] = x_ref[...] + _dot(mixed, wo_ref[...])


def _mix(hglu, o2, gates, x2, wdw, bdw, lng, lnb, wco, wao, wo, *, S, tm):
    T, D = x2.shape
    full = lambda a: pl.BlockSpec(a.shape, lambda i: (0, 0))
    row = lambda n: pl.BlockSpec((tm, n), lambda i: (i, 0))
    per = tm // HALO
    halo = pl.BlockSpec((HALO, D), lambda i: (jnp.maximum(i * per - 1, 0), 0))
    return pl.pallas_call(
        functools.partial(_mix_kernel, tm=tm, blocks_per_seq=S // tm, rows=64),
        grid=(T // tm,),
        in_specs=[row(D), halo, row(D), row(2 * D), row(D), full(wdw), full(bdw), full(lng), full(lnb),
                  full(wco), full(wao), full(wo)],
        out_specs=row(D),
        out_shape=jax.ShapeDtypeStruct((T, D), F32),
        scratch_shapes=[pltpu.VMEM((SUB, HALO + tm, D), F32), pltpu.VMEM((tm, D), F32)],
        compiler_params=_cparams(("parallel",)),
        name="mix",
    )(hglu, hglu, o2, gates, x2, wdw, bdw, lng, lnb, wco, wao, wo)


def _peer_prep_kernel(h1_ref, g_ref, wpqt_ref, sk_ref, xnt_ref, st_ref):
    tm = h1_ref.shape[0]
    xn32 = _rms(h1_ref[...], g_ref[...])
    xnt_ref[...] = xn32.T.astype(BF16)
    xn = xn32.astype(BF16)
    qt = _dot_nt(wpqt_ref[...], xn).astype(BF16)
    groups = N_KEYS // SUB
    for hp in range(2 * PEER_HEADS):
        rs = slice(hp * N_KEYS, (hp + 1) * N_KEYS)
        s = _dot(sk_ref[hp], qt[rs, :])
        for a in range(tm // LANE):
            st_ref[hp * groups:(hp + 1) * groups, a * SUB:(a + 1) * SUB, :] = (
                s[:, a * LANE:(a + 1) * LANE].reshape(groups, SUB, LANE))


def _peer_prep(h1, g, wpqt, sk, *, tm):
    T, D = h1.shape
    G = wpqt.shape[0] // SUB
    return pl.pallas_call(
        _peer_prep_kernel,
        grid=(T // tm,),
        in_specs=[pl.BlockSpec((tm, D), lambda i: (i, 0)), pl.BlockSpec(g.shape, lambda i: (0, 0)),
                  pl.BlockSpec(wpqt.shape, lambda i: (0, 0)), pl.BlockSpec(sk.shape, lambda i: (0, 0, 0))],
        out_specs=[pl.BlockSpec((D, tm), lambda i: (0, i)),
                   pl.BlockSpec((G, tm // LANE * SUB, LANE), lambda i: (0, i, 0))],
        out_shape=[jax.ShapeDtypeStruct((D, T), BF16), jax.ShapeDtypeStruct((G, T // LANE * SUB, LANE), F32)],
        compiler_params=_cparams(("parallel",)),
        name="peer_prep",
    )(h1, g, wpqt, sk)


def _tree(op, xs):
    xs = list(xs)
    while len(xs) > 1:
        xs = [op(xs[i], xs[i + 1]) for i in range(0, len(xs) - 1, 2)] + ([xs[-1]] if len(xs) % 2 else [])
    return xs[0]


def _first_index(xs, m, chains):
    n = len(xs)
    per = n // chains
    heads = []
    for c in range(chains):
        idx = jnp.full(m.shape, float(n), F32)
        for i in reversed(range(c * per, (c + 1) * per)):
            idx = jnp.where(xs[i] == m, float(i), idx)
        heads.append(idx)
    return _tree(jnp.minimum, heads)


def _peer_topk_kernel(st_ref, r2_ref, lc_ref, e1_ref, e2_ref, work, rank2, vals, idx1):
    K = PEER_TOPK

    def rows(k):
        return k // SUB, pl.ds(k % SUB, SUB, stride=SUB)

    def score(p, k):
        g, r = rows(p * N_KEYS + k)
        return st_ref[g, r, :]

    for p in range(2):
        for k in range(N_KEYS):
            work[p, k] = score(p, k)
    for k in range(N_KEYS):
        rank2[k] = jnp.full((SUB, LANE), float(K), F32)

    def extract(a, carry):
        af = lax.convert_element_type(a, F32)
        for p in range(2):
            w = [work[p, k] for k in range(N_KEYS)]
            m = _tree(jnp.maximum, w)
            idx = _first_index(w, m, 8)
            vals[p, a] = m
            if p == 0:
                idx1[a] = idx
            for k in range(N_KEYS):
                sel = idx == float(k)
                work[p, k] = jnp.where(sel, -jnp.inf, w[k])
                if p == 1:
                    rank2[k] = jnp.where(sel, af, rank2[k])
        return carry

    lax.fori_loop(0, K, extract, 0)

    v1 = [vals[0, a] for a in range(K)]
    v2 = [vals[1, b] for b in range(K)]
    plen = [jnp.zeros((SUB, LANE), F32) for _ in range(K)]
    front = [v1[a] + v2[0] for a in range(K)]
    for _ in range(K):
        m = _tree(jnp.maximum, front)
        a_star = _first_index(front, m, 2)
        lsel = jnp.zeros((SUB, LANE), F32)
        v1sel = jnp.zeros((SUB, LANE), F32)
        sels = []
        for a in range(K):
            sel = a_star == float(a)
            sels.append(sel)
            plen[a] = plen[a] + jnp.where(sel, 1.0, 0.0)
            lsel = jnp.where(sel, plen[a], lsel)
            v1sel = jnp.where(sel, v1[a], v1sel)
        v2next = jnp.zeros((SUB, LANE), F32)
        for b in range(1, K):
            v2next = jnp.where(lsel == float(b), v2[b], v2next)
        fnew = jnp.where(lsel >= float(K), -jnp.inf, v1sel + v2next)
        for a in range(K):
            front[a] = jnp.where(sels[a], fnew, front[a])
    e1s = [jnp.exp(v1[a] - v1[0]) for a in range(K)]
    z = jnp.zeros((SUB, LANE), F32)
    for b in range(K):
        inner = jnp.zeros((SUB, LANE), F32)
        for a in range(K):
            inner = inner + jnp.where(plen[a] > float(b), e1s[a], 0.0)
        z = z + jnp.exp(v2[b] - v2[0]) * inner
    rz = 1.0 / z
    ids = [idx1[a] for a in range(K)]
    for k in range(N_KEYS):
        g, r = rows(k)
        lc = jnp.zeros((SUB, LANE), F32)
        for a in range(K):
            lc = jnp.where(ids[a] == float(k), plen[a], lc)
        lc_ref[0, g, r, :] = lc
        e1_ref[0, g, r, :] = jnp.exp(score(0, k) - v1[0]) * rz
        r2_ref[0, g, r, :] = rank2[k]
        e2_ref[0, g, r, :] = jnp.exp(score(1, k) - v2[0])


def _peer_topk(st, *, T, tokens):
    groups = N_KEYS // SUB
    rows_per_step = tokens // LANE * SUB
    tab = pl.BlockSpec((1, groups, rows_per_step, LANE), lambda t, h: (h, 0, t, 0))
    shp = jax.ShapeDtypeStruct((PEER_HEADS, groups, T // LANE * SUB, LANE), F32)
    vreg = (SUB, LANE)
    return pl.pallas_call(
        _peer_topk_kernel,
        grid=(T // tokens, PEER_HEADS),
        in_specs=[pl.BlockSpec((2 * groups, rows_per_step, LANE), lambda t, h: (h, t, 0))],
        out_specs=[tab, tab, tab, tab],
        out_shape=[shp, shp, shp, shp],
        scratch_shapes=[pltpu.VMEM((2, N_KEYS) + vreg, F32), pltpu.VMEM((N_KEYS,) + vreg, F32),
                        pltpu.VMEM((2, PEER_TOPK) + vreg, F32), pltpu.VMEM((PEER_TOPK,) + vreg, F32)],
        compiler_params=_cparams(("parallel", "parallel")),
        name="peer_topk",
    )(st)


def _gelu(a):
    return 0.5 * a * (1.0 + lax.erf(a * (1.0 / math.sqrt(2.0))))


def _interleave(major, minor):
    out, done = [], 0
    for n, item in enumerate(major):
        out.append(item)
        want = (n + 1) * len(minor) // len(major)
        out += minor[done:want]
        done = want
    return out


def _peer_dense_kernel(xnt_ref, u_ref, vt_ref, r2_ref, lc_ref, e1_ref, e2_ref, out_ref, r2b, e2b, w_scr, act_scr,
                       *, nchunks, nwork):
    s = pl.program_id(0)
    D, tb = xnt_ref.shape
    ec = u_ref.shape[0]
    tiles = ec // N_KEYS
    nl = tb // LANE
    packed = (N_KEYS // BF16_ROWS, BF16_ROWS, LANE)
    w_item = jnp.clip(s - 1, 0, nwork - 1)
    c_item = jnp.clip(s - 2, 0, nwork - 1)
    chunk = w_item % nchunks

    @pl.when(s == 0)
    def _():
        w_scr[...] = jnp.zeros(w_scr.shape, BF16)
        act_scr[...] = jnp.zeros(act_scr.shape, BF16)

    @pl.when(chunk == 0)
    def _():
        for h in range(PEER_HEADS):
            for c in range(nl):
                r2b[h, c] = r2_ref[h, :, c * SUB:(c + 1) * SUB, :].reshape(packed).astype(BF16)
                e2b[h, c] = e2_ref[h, :, c * SUB:(c + 1) * SUB, :].reshape(packed).astype(BF16)

    @pl.when(c_item % nchunks == 0)
    def _():
        out_ref[...] = jnp.zeros(out_ref.shape, F32)

    cur = s % 2
    prev = 1 - cur
    zero = jnp.zeros((), BF16)
    halves = 2
    hw = tb // halves
    quarters = 4
    dp = D // quarters
    share = 2

    def scores(n):
        cs = slice(n * hw, (n + 1) * hw)
        act_scr[cur, :, cs] = _dot(u_ref[...], xnt_ref[:, cs]).astype(BF16)

    def combine(n):
        rs = slice(n * dp, (n + 1) * dp)
        out_ref[rs, :] += _dot(vt_ref[rs, :], w_scr[cur])

    def weights(i0, c):
        gs = [jnp.zeros(packed, BF16) for _ in range(share)]
        for h in range(PEER_HEADS):
            r2 = r2b[h, c]
            e2 = e2b[h, c]
            for n in range(share):
                row = c * SUB + i0 + n
                lc = jnp.broadcast_to(lc_ref[h, pl.ds(chunk, 1), row, :], (BF16_ROWS, LANE)).astype(BF16)
                e1 = jnp.broadcast_to(e1_ref[h, pl.ds(chunk, 1), row, :], (BF16_ROWS, LANE)).astype(BF16)
                gs[n] = gs[n] + jnp.where(r2 < lc[None], e2, zero) * e1[None]
        cs = slice(c * LANE, (c + 1) * LANE)
        for n in range(share):
            rs = slice((i0 + n) * N_KEYS, (i0 + n + 1) * N_KEYS)
            a = _gelu(act_scr[prev, rs, cs])
            w_scr[prev, rs, cs] = gs[n].reshape(N_KEYS, LANE) * a

    mxu = [functools.partial(combine, 0), functools.partial(scores, 0), functools.partial(combine, 1),
           functools.partial(combine, 2), functools.partial(scores, 1), functools.partial(combine, 3)]
    vpu = [functools.partial(weights, i0, c) for i0 in range(0, tiles, share) for c in range(nl)]
    for stage in _interleave(mxu, vpu):
        stage()


def _peer_dense(xnt, u, vt, r2, lc, e1, e2, *, tb, ec):
    D, T = xnt.shape
    E = u.shape[0]
    assert ec == N_KEYS * SUB
    nchunks = E // ec
    nwork = (T // tb) * nchunks
    groups = N_KEYS // SUB
    nl = tb // LANE
    s_item = lambda s: jnp.minimum(s, nwork - 1)
    w_item = lambda s: jnp.clip(s - 1, 0, nwork - 1)
    c_item = lambda s: jnp.clip(s - 2, 0, nwork - 1)
    tab = pl.BlockSpec((PEER_HEADS, groups, nl * SUB, LANE), lambda s: (0, 0, w_item(s) // nchunks, 0))
    packed = (PEER_HEADS, nl, N_KEYS // BF16_ROWS, BF16_ROWS, LANE)
    return pl.pallas_call(
        functools.partial(_peer_dense_kernel, nchunks=nchunks, nwork=nwork),
        grid=(nwork + 2,),
        in_specs=[pl.BlockSpec((D, tb), lambda s: (0, s_item(s) // nchunks)),
                  pl.BlockSpec((ec, D), lambda s: (s_item(s) % nchunks, 0)),
                  pl.BlockSpec((D, ec), lambda s: (0, c_item(s) % nchunks)),
                  tab, tab, tab, tab],
        out_specs=pl.BlockSpec((D, tb), lambda s: (0, c_item(s) // nchunks)),
        out_shape=jax.ShapeDtypeStruct((D, T), F32),
        scratch_shapes=[pltpu.VMEM(packed, BF16), pltpu.VMEM(packed, BF16), pltpu.VMEM((2, ec, tb), BF16),
                        pltpu.VMEM((2, ec, tb), BF16)],
        compiler_params=_cparams(("arbitrary",)),
        name="peer_dense",
    )(xnt, u, vt, r2, lc, e1, e2)


def _final_kernel(pt_ref, h1_ref, g_ref, out_ref, *, normalize):
    h2 = h1_ref[...] + pt_ref[...].T
    out_ref[...] = _rms(h2, g_ref[...]) if normalize else h2


def _final(pt, h1, g, *, tb, normalize):
    T, D = h1.shape
    return pl.pallas_call(
        functools.partial(_final_kernel, normalize=normalize),
        grid=(T // tb,),
        in_specs=[pl.BlockSpec((D, tb), lambda i: (0, i)), pl.BlockSpec((tb, D), lambda i: (i, 0)),
                  pl.BlockSpec(g.shape, lambda i: (0, 0))],
        out_specs=pl.BlockSpec((tb, D), lambda i: (i, 0)),
        out_shape=jax.ShapeDtypeStruct((T, D), F32),
        compiler_params=_cparams(("parallel",)),
        name="final_norm",
    )(pt, h1, g)


def _rot_cols(w):
    half = w.shape[-1] // 2
    return jnp.concatenate([-w[..., half:], w[..., :half]], axis=-1)


def _layer(h2d, pos2, invf, p, *, B, S):
    T, D = h2d.shape
    w_in = p["w_in"]
    o0, o1, o2, o3 = D, 2 * D, 2 * D + Q_LORA, 2 * D + Q_LORA + KV_LORA + QK_ROPE
    w_kr = w_in[:, o2 + KV_LORA:o3]
    zpad = jnp.zeros((D, LANE - QK_ROPE), F32)
    wkv = jnp.concatenate([w_in[:, o2:o2 + KV_LORA], w_kr, zpad, _rot_cols(w_kr), zpad], axis=1)
    hglu, cq, ckv, gates = _in_proj(
        h2d, p["mix_norm_g"][None, :], w_in[:, :o0].astype(BF16), w_in[:, o0:o1].astype(BF16),
        w_in[:, o1:o2].astype(BF16), wkv.astype(BF16), w_in[:, o3:].astype(BF16), p["b_gate"][None, :],
        tm=TILES.rows)

    w_uq = p["w_uq"].reshape(Q_LORA, N_HEADS, QK_NOPE + QK_ROPE)
    zq = jnp.zeros((Q_LORA, N_HEADS, LANE - QK_ROPE), F32)
    wqa = jnp.concatenate([w_uq, zq], axis=-1).reshape(Q_LORA, N_HEADS * QK_PAD)
    wqr = jnp.concatenate([_rot_cols(w_uq[..., QK_NOPE:]), zq], axis=-1).reshape(Q_LORA, N_HEADS * LANE)
    w_ukv = p["w_ukv"].reshape(KV_LORA, N_HEADS, QK_NOPE + V_DIM)
    wk = w_ukv[..., :QK_NOPE].reshape(KV_LORA, N_HEADS * QK_NOPE)
    wv = w_ukv[..., QK_NOPE:].reshape(KV_LORA, N_HEADS * V_DIM)
    q, k, v = _mla_proj(cq, ckv, pos2, invf, p["q_norm_g"][None, :], wqa.astype(BF16), wqr.astype(BF16),
                        p["kv_norm_g"][None, :], wk.astype(BF16), wv.T.astype(BF16), B=B, S=S, tm=TILES.rows)
    o = _flash_attn(q, k, v, bq=TILES.attn_block, heads=TILES.attn_heads)

    h1 = _mix(hglu, o.reshape(T, D), gates, h2d, p["w_dw"].reshape(CONV_WIDTH, D), p["b_dw"][None, :],
              p["conv_ln_g"][None, :], p["conv_ln_b"][None, :], p["w_conv_out"].astype(BF16),
              p["w_attn_out"].astype(BF16), p["w_o"].astype(BF16), S=S, tm=TILES.conv_rows)

    sk = p["peer_sub_keys"].reshape(2 * PEER_HEADS, N_KEYS, -1).astype(BF16)
    xnt, st = _peer_prep(h1, p["ffn_norm_g"][None, :], p["w_peer_q"].T.astype(BF16), sk, tm=TILES.rows)
    r2, lc, e1, e2 = _peer_topk(st, T=T, tokens=SUB * LANE)
    pt = _peer_dense(xnt, p["peer_u"].astype(BF16), p["peer_v"].T.astype(BF16), r2, lc, e1, e2,
                     tb=TILES.rows, ec=N_KEYS * SUB)
    return h1, pt


def kernel(x, positions, mix_norm_g, w_in, b_gate, w_dw, b_dw, conv_ln_g, conv_ln_b, w_conv_out, q_norm_g, w_uq,
           kv_norm_g, w_ukv, w_attn_out, w_o, ffn_norm_g, w_peer_q, peer_sub_keys, peer_u, peer_v, final_norm_g):
    B, S, D = x.shape
    T = B * S
    stacked = dict(mix_norm_g=mix_norm_g, w_in=w_in, b_gate=b_gate, w_dw=w_dw, b_dw=b_dw, conv_ln_g=conv_ln_g,
                   conv_ln_b=conv_ln_b, w_conv_out=w_conv_out, q_norm_g=q_norm_g, w_uq=w_uq, kv_norm_g=kv_norm_g,
                   w_ukv=w_ukv, w_attn_out=w_attn_out, w_o=w_o, ffn_norm_g=ffn_norm_g, w_peer_q=w_peer_q,
                   peer_sub_keys=peer_sub_keys, peer_u=peer_u, peer_v=peer_v)
    depth = w_in.shape[0]
    half = QK_ROPE // 2
    inv_freq = ROPE_BASE ** (-jnp.arange(half, dtype=F32) / half)
    invf = jnp.concatenate([inv_freq, inv_freq, jnp.zeros((LANE - QK_ROPE,), F32)])[None, :]
    pos2 = positions.reshape(T, 1)
    h = x.reshape(T, D)
    for l in range(depth):
        p = {name: a[l] for name, a in stacked.items()}
        h1, pt = _layer(h, pos2, invf, p, B=B, S=S)
        last = l + 1 == depth
        h = _final(pt, h1, final_norm_g[None, :], tb=TILES.rows, normalize=last)
    return h.reshape(B, S, D)
```

```python
import functools
import math
from typing import NamedTuple

import jax
import jax.numpy as jnp
from jax import lax
from jax.experimental import pallas as pl
from jax.experimental.pallas import tpu as pltpu

F32 = jnp.float32
BF16 = jnp.bfloat16

EPS = 1e-6
CONV_WIDTH = 31
N_HEADS = 8
QK_NOPE = 128
QK_ROPE = 64
V_DIM = 128
Q_LORA = 256
KV_LORA = 128
ROPE_BASE = 10000.0
N_KEYS = 128
PEER_HEADS = 8
PEER_TOPK = 16
QK_PAD = 256
LANE = 128
SUB = 8
BF16_ROWS = 16
HALO = 32
VMEM_LIMIT = 56 * 1024 * 1024


class _Tiles(NamedTuple):
    rows: int
    conv_rows: int
    attn_block: int
    attn_heads: int
    experts: int


TILES = _Tiles(rows=512, conv_rows=256, attn_block=512, attn_heads=4, experts=2048)


def _cparams(sem):
    return pltpu.CompilerParams(dimension_semantics=sem, vmem_limit_bytes=VMEM_LIMIT)


def _dot(a, b):
    return jnp.dot(a, b, preferred_element_type=F32)


def _dot_nt(a, b):
    return lax.dot_general(a, b, (((1,), (1,)), ((), ())), preferred_element_type=F32)


def _sigmoid(x):
    return 1.0 / (1.0 + jnp.exp(-x))


def _rms(x, g):
    return x * lax.rsqrt(jnp.mean(x * x, axis=-1, keepdims=True) + EPS) * g


def _inproj_kernel(x_ref, g_ref, wv_ref, wg_ref, wq_ref, wkv_ref, wgl_ref, bg_ref,
                   hglu_ref, cq_ref, ckv_ref, gates_ref):
    a = _rms(x_ref[...], g_ref[...]).astype(BF16)
    val = _dot(a, wv_ref[...])
    gate = _dot(a, wg_ref[...])
    hglu_ref[...] = val * _sigmoid(gate)
    cq_ref[...] = _dot(a, wq_ref[...])
    ckv_ref[...] = _dot(a, wkv_ref[...])
    gates_ref[...] = _sigmoid(_dot(a, wgl_ref[...]) + bg_ref[...]).astype(BF16)


def _in_proj(x2, g, wv, wg, wq, wkv, wgl, bg, *, tm):
    T, D = x2.shape
    full = lambda a: pl.BlockSpec(a.shape, lambda i: (0, 0))
    row = lambda n: pl.BlockSpec((tm, n), lambda i: (i, 0))
    return pl.pallas_call(
        _inproj_kernel,
        grid=(T // tm,),
        in_specs=[row(D), full(g), full(wv), full(wg), full(wq), full(wkv), full(wgl), full(bg)],
        out_specs=[row(wv.shape[1]), row(wq.shape[1]), row(wkv.shape[1]), row(wgl.shape[1])],
        out_shape=[jax.ShapeDtypeStruct((T, wv.shape[1]), F32),
                   jax.ShapeDtypeStruct((T, wq.shape[1]), F32),
                   jax.ShapeDtypeStruct((T, wkv.shape[1]), F32),
                   jax.ShapeDtypeStruct((T, wgl.shape[1]), BF16)],
        compiler_params=_cparams(("parallel",)),
        name="in_proj",
    )(x2, g, wv, wg, wq, wkv, wgl, bg)


def _mla_proj_kernel(cq_ref, ckv_ref, pos_ref, invf_ref, qg_ref, wqa_ref, wqr_ref, kvg_ref, wk_ref, wv_ref,
                     q_ref, k_ref, v_ref, *, scale):
    qn = _rms(cq_ref[...], qg_ref[...]).astype(BF16)
    ckv_full = ckv_ref[...]
    kvn = _rms(ckv_full[:, :KV_LORA], kvg_ref[...]).astype(BF16)
    ang = pos_ref[...].astype(F32) * invf_ref[...]
    cosv = jnp.cos(ang)
    sinv = jnp.sin(ang)
    kr = (ckv_full[:, KV_LORA:KV_LORA + LANE] * cosv + ckv_full[:, KV_LORA + LANE:] * sinv).astype(BF16)
    qa = _dot(qn, wqa_ref[...])
    qr = _dot(qn, wqr_ref[...])
    kn = _dot(kvn, wk_ref[...])
    vt = _dot_nt(wv_ref[...], kvn)
    for h in range(N_HEADS):
        q_ref[0, h, :, 0:LANE] = (qa[:, h * QK_PAD:h * QK_PAD + LANE] * scale).astype(BF16)
        q_rope = qa[:, h * QK_PAD + LANE:(h + 1) * QK_PAD] * cosv + qr[:, h * LANE:(h + 1) * LANE] * sinv
        q_ref[0, h, :, LANE:QK_PAD] = (q_rope * scale).astype(BF16)
        k_ref[0, h, :, 0:LANE] = kn[:, h * LANE:(h + 1) * LANE].astype(BF16)
        k_ref[0, h, :, LANE:QK_PAD] = kr
        v_ref[0, h, :, :] = vt[h * V_DIM:(h + 1) * V_DIM, :].astype(BF16)


def _mla_proj(cq, ckv, pos2, invf, qg, wqa, wqr, kvg, wk, wv, *, B, S, tm):
    T = cq.shape[0]
    nb = S // tm
    full = lambda a: pl.BlockSpec(a.shape, lambda i: (0, 0))
    row = lambda n: pl.BlockSpec((tm, n), lambda i: (i, 0))
    head = lambda n: pl.BlockSpec((1, N_HEADS, tm, n), lambda i: (i // nb, 0, i % nb, 0))
    head_t = pl.BlockSpec((1, N_HEADS, V_DIM, tm), lambda i: (i // nb, 0, 0, i % nb))
    scale = math.log2(math.e) / math.sqrt(QK_NOPE + QK_ROPE)
    return pl.pallas_call(
        functools.partial(_mla_proj_kernel, scale=scale),
        grid=(T // tm,),
        in_specs=[row(cq.shape[1]), row(ckv.shape[1]), row(1), full(invf), full(qg), full(wqa), full(wqr),
                  full(kvg), full(wk), full(wv)],
        out_specs=[head(QK_PAD), head(QK_PAD), head_t],
        out_shape=[jax.ShapeDtypeStruct((B, N_HEADS, S, QK_PAD), BF16),
                   jax.ShapeDtypeStruct((B, N_HEADS, S, QK_PAD), BF16),
                   jax.ShapeDtypeStruct((B, N_HEADS, V_DIM, S), BF16)],
        compiler_params=_cparams(("parallel",)),
        name="mla_proj",
    )(cq, ckv, pos2, invf, qg, wqa, wqr, kvg, wk, wv)


def _flash_kernel(q_ref, k_ref, vt_ref, o_ref, acc_scr, *, bq, bk, heads):
    qi = pl.program_id(2)
    acc_scr[...] = jnp.zeros(acc_scr.shape, F32)

    def step(ki, carry, diagonal):
        off = pl.multiple_of(ki * bk, bk)
        out = []
        for hh in range(heads):
            m_prev, l_prev = carry[2 * hh], carry[2 * hh + 1]
            st = _dot_nt(k_ref[0, hh, pl.ds(off, bk), :], q_ref[0, hh])
            if diagonal:
                kpos = lax.broadcasted_iota(jnp.int32, (bk, bq), 0)
                qpos = lax.broadcasted_iota(jnp.int32, (bk, bq), 1)
                st = jnp.where(kpos <= qpos, st, -1e30)
            m_new = jnp.maximum(m_prev, jnp.max(st, axis=0, keepdims=True))
            p = jnp.exp2(st - m_new)
            alpha = jnp.exp2(m_prev - m_new)
            l_new = alpha * l_prev + jnp.sum(p, axis=0, keepdims=True)
            pv = _dot(vt_ref[0, hh, :, pl.ds(off, bk)], p.astype(BF16))
            acc_scr[hh] = alpha * acc_scr[hh] + pv
            out += [m_new, l_new]
        return tuple(out)

    init = (jnp.full((1, bq), -1e30, F32), jnp.zeros((1, bq), F32)) * heads
    pairs = qi // 2
    carry = lax.fori_loop(0, pairs, lambda kp, c: step(2 * kp + 1, step(2 * kp, c, False), False), init)
    carry = lax.fori_loop(2 * pairs, qi, lambda ki, c: step(ki, c, False), carry)
    carry = step(qi, carry, True)
    for hh in range(heads):
        o_ref[0, :, hh * V_DIM:(hh + 1) * V_DIM] = (acc_scr[hh] / carry[2 * hh + 1]).T.astype(BF16)


def _flash_attn(q, k, vt, *, bq, heads):
    B, H, S, _ = q.shape
    return pl.pallas_call(
        functools.partial(_flash_kernel, bq=bq, bk=bq, heads=heads),
        grid=(B, H // heads, S // bq),
        in_specs=[pl.BlockSpec((1, heads, bq, QK_PAD), lambda b, h, i: (b, h, i, 0)),
                  pl.BlockSpec((1, heads, S, QK_PAD), lambda b, h, i: (b, h, 0, 0)),
                  pl.BlockSpec((1, heads, V_DIM, S), lambda b, h, i: (b, h, 0, 0))],
        out_specs=pl.BlockSpec((1, bq, heads * V_DIM), lambda b, h, i: (b, i, h)),
        out_shape=jax.ShapeDtypeStruct((B, S, H * V_DIM), BF16),
        scratch_shapes=[pltpu.VMEM((heads, V_DIM, bq), F32)],
        compiler_params=_cparams(("parallel", "parallel", "arbitrary")),
        name="flash_attn",
    )(q, k, vt)


def _mix_kernel(hc_ref, hp_ref, o_ref, gates_ref, x_ref, wdw_ref, bdw_ref, lng_ref, lnb_ref,
                wco_ref, wao_ref, wo_ref, h1_ref, hbuf, cbuf, *, tm, blocks_per_seq, rows):
    i = pl.program_id(0)
    first = (i % blocks_per_seq) == 0
    hbuf[0, 0:HALO, :] = jnp.where(first, 0.0, hp_ref[...])
    hbuf[0, HALO:HALO + tm, :] = hc_ref[...]
    D = hc_ref.shape[1]
    span = HALO + tm - SUB
    for s in range(1, SUB):
        hbuf[s, 0:span, :] = hbuf[0, s:s + span, :]
    base = HALO - (CONV_WIDTH - 1)
    for c in range(D // LANE):
        cs = slice(c * LANE, (c + 1) * LANE)
        for r in range(tm // rows):
            acc = jnp.broadcast_to(bdw_ref[:, cs], (rows, LANE))
            for kk in range(CONV_WIDTH):
                shift = (base + kk) % SUB
                start = r * rows + base + kk - shift
                acc = acc + wdw_ref[kk:kk + 1, cs] * hbuf[shift, start:start + rows, cs]
            cbuf[r * rows:(r + 1) * rows, cs] = acc
    y = cbuf[...]
    mu = jnp.mean(y, axis=-1, keepdims=True)
    yc = y - mu
    var = jnp.mean(yc * yc, axis=-1, keepdims=True)
    yn = yc * lax.rsqrt(var + EPS) * lng_ref[...] + lnb_ref[...]
    act = (yn * _sigmoid(yn)).astype(BF16)
    y_conv = _dot(act, wco_ref[...])
    y_attn = _dot(o_ref[...], wao_ref[...])
    gts = gates_ref[...].astype(F32)
    mixed = (gts[:, :D] * y_conv + gts[:, D:] * y_attn).astype(BF16)
    h1_ref[...] = x_ref[...] + _dot(mixed, wo_ref[...])


def _mix(hglu, o2, gates, x2, wdw, bdw, lng, lnb, wco, wao, wo, *, S, tm):
    T, D = x2.shape
    full = lambda a: pl.BlockSpec(a.shape, lambda i: (0, 0))
    row = lambda n: pl.BlockSpec((tm, n), lambda i: (i, 0))
    per = tm // HALO
    halo = pl.BlockSpec((HALO, D), lambda i: (jnp.maximum(i * per - 1, 0), 0))
    return pl.pallas_call(
        functools.partial(_mix_kernel, tm=tm, blocks_per_seq=S // tm, rows=64),
        grid=(T // tm,),
        in_specs=[row(D), halo, row(D), row(2 * D), row(D), full(wdw), full(bdw), full(lng), full(lnb),
                  full(wco), full(wao), full(wo)],
        out_specs=row(D),
        out_shape=jax.ShapeDtypeStruct((T, D), F32),
        scratch_shapes=[pltpu.VMEM((SUB, HALO + tm, D), F32), pltpu.VMEM((tm, D), F32)],
        compiler_params=_cparams(("parallel",)),
        name="mix",
    )(hglu, hglu, o2, gates, x2, wdw, bdw, lng, lnb, wco, wao, wo)


def _peer_prep_kernel(h1_ref, g_ref, wpqt_ref, sk_ref, xnt_ref, st_ref):
    tm = h1_ref.shape[0]
    xn32 = _rms(h1_ref[...], g_ref[...])
    xnt_ref[...] = xn32.T.astype(BF16)
    xn = xn32.astype(BF16)
    qt = _dot_nt(wpqt_ref[...], xn).astype(BF16)
    groups = N_KEYS // SUB
    for hp in range(2 * PEER_HEADS):
        rs = slice(hp * N_KEYS, (hp + 1) * N_KEYS)
        s = _dot(sk_ref[hp], qt[rs, :])
        for a in range(tm // LANE):
            st_ref[hp * groups:(hp + 1) * groups, a * SUB:(a + 1) * SUB, :] = (
                s[:, a * LANE:(a + 1) * LANE].reshape(groups, SUB, LANE))


def _peer_prep(h1, g, wpqt, sk, *, tm):
    T, D = h1.shape
    G = wpqt.shape[0] // SUB
    return pl.pallas_call(
        _peer_prep_kernel,
        grid=(T // tm,),
        in_specs=[pl.BlockSpec((tm, D), lambda i: (i, 0)), pl.BlockSpec(g.shape, lambda i: (0, 0)),
                  pl.BlockSpec(wpqt.shape, lambda i: (0, 0)), pl.BlockSpec(sk.shape, lambda i: (0, 0, 0))],
        out_specs=[pl.BlockSpec((D, tm), lambda i: (0, i)),
                   pl.BlockSpec((G, tm // LANE * SUB, LANE), lambda i: (0, i, 0))],
        out_shape=[jax.ShapeDtypeStruct((D, T), BF16), jax.ShapeDtypeStruct((G, T // LANE * SUB, LANE), F32)],
        compiler_params=_cparams(("parallel",)),
        name="peer_prep",
    )(h1, g, wpqt, sk)


def _tree(op, xs):
    xs = list(xs)
    while len(xs) > 1:
        xs = [op(xs[i], xs[i + 1]) for i in range(0, len(xs) - 1, 2)] + ([xs[-1]] if len(xs) % 2 else [])
    return xs[0]


def _first_index(xs, m, chains):
    n = len(xs)
    per = n // chains
    heads = []
    for c in range(chains):
        idx = jnp.full(m.shape, float(n), F32)
        for i in reversed(range(c * per, (c + 1) * per)):
            idx = jnp.where(xs[i] == m, float(i), idx)
        heads.append(idx)
    return _tree(jnp.minimum, heads)


def _peer_topk_kernel(st_ref, r2_ref, lc_ref, e1_ref, e2_ref, work, rank2, vals, idx1):
    K = PEER_TOPK

    def rows(k):
        return k // SUB, pl.ds(k % SUB, SUB, stride=SUB)

    def score(p, k):
        g, r = rows(p * N_KEYS + k)
        return st_ref[g, r, :]

    for p in range(2):
        for k in range(N_KEYS):
            work[p, k] = score(p, k)
    for k in range(N_KEYS):
        rank2[k] = jnp.full((SUB, LANE), float(K), F32)

    def extract(a, carry):
        af = lax.convert_element_type(a, F32)
        for p in range(2):
            w = [work[p, k] for k in range(N_KEYS)]
            m = _tree(jnp.maximum, w)
            idx = _first_index(w, m, 8)
            vals[p, a] = m
            if p == 0:
                idx1[a] = idx
            for k in range(N_KEYS):
                sel = idx == float(k)
                work[p, k] = jnp.where(sel, -jnp.inf, w[k])
                if p == 1:
                    rank2[k] = jnp.where(sel, af, rank2[k])
        return carry

    lax.fori_loop(0, K, extract, 0)

    v1 = [vals[0, a] for a in range(K)]
    v2 = [vals[1, b] for b in range(K)]
    plen = [jnp.zeros((SUB, LANE), F32) for _ in range(K)]
    front = [v1[a] + v2[0] for a in range(K)]
    for _ in range(K):
        m = _tree(jnp.maximum, front)
        a_star = _first_index(front, m, 2)
        lsel = jnp.zeros((SUB, LANE), F32)
        v1sel = jnp.zeros((SUB, LANE), F32)
        sels = []
        for a in range(K):
            sel = a_star == float(a)
            sels.append(sel)
            plen[a] = plen[a] + jnp.where(sel, 1.0, 0.0)
            lsel = jnp.where(sel, plen[a], lsel)
            v1sel = jnp.where(sel, v1[a], v1sel)
        v2next = jnp.zeros((SUB, LANE), F32)
        for b in range(1, K):
            v2next = jnp.where(lsel == float(b), v2[b], v2next)
        fnew = jnp.where(lsel >= float(K), -jnp.inf, v1sel + v2next)
        for a in range(K):
            front[a] = jnp.where(sels[a], fnew, front[a])
    e1s = [jnp.exp(v1[a] - v1[0]) for a in range(K)]
    z = jnp.zeros((SUB, LANE), F32)
    for b in range(K):
        inner = jnp.zeros((SUB, LANE), F32)
        for a in range(K):
            inner = inner + jnp.where(plen[a] > float(b), e1s[a], 0.0)
        z = z + jnp.exp(v2[b] - v2[0]) * inner
    rz = 1.0 / z
    ids = [idx1[a] for a in range(K)]
    for k in range(N_KEYS):
        g, r = rows(k)
        lc = jnp.zeros((SUB, LANE), F32)
        for a in range(K):
            lc = jnp.where(ids[a] == float(k), plen[a], lc)
        lc_ref[0, g, r, :] = lc
        e1_ref[0, g, r, :] = jnp.exp(score(0, k) - v1[0]) * rz
        r2_ref[0, g, r, :] = rank2[k]
        e2_ref[0, g, r, :] = jnp.exp(score(1, k) - v2[0])


def _peer_topk(st, *, T, tokens):
    groups = N_KEYS // SUB
    rows_per_step = tokens // LANE * SUB
    tab = pl.BlockSpec((1, groups, rows_per_step, LANE), lambda t, h: (h, 0, t, 0))
    shp = jax.ShapeDtypeStruct((PEER_HEADS, groups, T // LANE * SUB, LANE), F32)
    vreg = (SUB, LANE)
    return pl.pallas_call(
        _peer_topk_kernel,
        grid=(T // tokens, PEER_HEADS),
        in_specs=[pl.BlockSpec((2 * groups, rows_per_step, LANE), lambda t, h: (h, t, 0))],
        out_specs=[tab, tab, tab, tab],
        out_shape=[shp, shp, shp, shp],
        scratch_shapes=[pltpu.VMEM((2, N_KEYS) + vreg, F32), pltpu.VMEM((N_KEYS,) + vreg, F32),
                        pltpu.VMEM((2, PEER_TOPK) + vreg, F32), pltpu.VMEM((PEER_TOPK,) + vreg, F32)],
        compiler_params=_cparams(("parallel", "parallel")),
        name="peer_topk",
    )(st)


def _gelu(a):
    return 0.5 * a * (1.0 + lax.erf(a * (1.0 / math.sqrt(2.0))))


def _interleave(major, minor):
    out, done = [], 0
    for n, item in enumerate(major):
        out.append(item)
        want = (n + 1) * len(minor) // len(major)
        out += minor[done:want]
        done = want
    return out


def _peer_dense_kernel(xnt_ref, u_ref, vt_ref, r2_ref, lc_ref, e1_ref, e2_ref, out_ref, r2b, e2b, w_scr, act_scr,
                       *, nchunks, nwork):
    s = pl.program_id(0)
    D, tb = xnt_ref.shape
    ec = u_ref.shape[0]
    tiles = ec // N_KEYS
    nl = tb // LANE
    packed = (N_KEYS // BF16_ROWS, BF16_ROWS, LANE)
    w_item = jnp.clip(s - 1, 0, nwork - 1)
    c_item = jnp.clip(s - 2, 0, nwork - 1)
    chunk = w_item % nchunks

    @pl.when(s == 0)
    def _():
        w_scr[...] = jnp.zeros(w_scr.shape, BF16)
        act_scr[...] = jnp.zeros(act_scr.shape, F32)

    @pl.when(chunk == 0)
    def _():
        for h in range(PEER_HEADS):
            for c in range(nl):
                r2b[h, c] = r2_ref[h, :, c * SUB:(c + 1) * SUB, :].reshape(packed).astype(BF16)
                e2b[h, c] = e2_ref[h, :, c * SUB:(c + 1) * SUB, :].reshape(packed).astype(BF16)

    @pl.when(c_item % nchunks == 0)
    def _():
        out_ref[...] = jnp.zeros(out_ref.shape, F32)

    cur = s % 2
    prev = 1 - cur
    zero = jnp.zeros((), BF16)
    halves = 2
    hw = tb // halves
    quarters = 4
    dp = D // quarters
    share = 2

    def scores(n):
        cs = slice(n * hw, (n + 1) * hw)
        act_scr[cur, :, cs] = _dot(u_ref[...], xnt_ref[:, cs])

    def combine(n):
        rs = slice(n * dp, (n + 1) * dp)
        out_ref[rs, :] += _dot(vt_ref[rs, :], w_scr[cur])

    def weights(i0, c):
        gs = [jnp.zeros(packed, BF16) for _ in range(share)]
        for h in range(PEER_HEADS):
            r2 = r2b[h, c]
            e2 = e2b[h, c]
            for n in range(share):
                grp = chunk * (tiles // SUB) + (i0 + n) // SUB
                row = c * SUB + (i0 + n) % SUB
                lc = jnp.broadcast_to(lc_ref[h, pl.ds(grp, 1), row, :], (BF16_ROWS, LANE)).astype(BF16)
                e1 = jnp.broadcast_to(e1_ref[h, pl.ds(grp, 1), row, :], (BF16_ROWS, LANE)).astype(BF16)
                gs[n] = gs[n] + jnp.where(r2 < lc[None], e2, zero) * e1[None]
        cs = slice(c * LANE, (c + 1) * LANE)
        for n in range(share):
            rs = slice((i0 + n) * N_KEYS, (i0 + n + 1) * N_KEYS)
            a = _gelu(act_scr[prev, rs, cs].astype(BF16))
            w_scr[prev, rs, cs] = gs[n].reshape(N_KEYS, LANE) * a

    mxu = [functools.partial(combine, 0), functools.partial(scores, 0), functools.partial(combine, 1),
           functools.partial(combine, 2), functools.partial(scores, 1), functools.partial(combine, 3)]
    vpu = [functools.partial(weights, i0, c) for i0 in range(0, tiles, share) for c in range(nl)]
    for stage in _interleave(mxu, vpu):
        stage()


def _peer_dense(xnt, u, vt, r2, lc, e1, e2, *, tb, ec):
    D, T = xnt.shape
    E = u.shape[0]
    assert ec % (N_KEYS * SUB) == 0
    nchunks = E // ec
    nwork = (T // tb) * nchunks
    groups = N_KEYS // SUB
    nl = tb // LANE
    s_item = lambda s: jnp.minimum(s, nwork - 1)
    w_item = lambda s: jnp.clip(s - 1, 0, nwork - 1)
    c_item = lambda s: jnp.clip(s - 2, 0, nwork - 1)
    tab = pl.BlockSpec((PEER_HEADS, groups, nl * SUB, LANE), lambda s: (0, 0, w_item(s) // nchunks, 0))
    packed = (PEER_HEADS, nl, N_KEYS // BF16_ROWS, BF16_ROWS, LANE)
    return pl.pallas_call(
        functools.partial(_peer_dense_kernel, nchunks=nchunks, nwork=nwork),
        grid=(nwork + 2,),
        in_specs=[pl.BlockSpec((D, tb), lambda s: (0, s_item(s) // nchunks)),
                  pl.BlockSpec((ec, D), lambda s: (s_item(s) % nchunks, 0)),
                  pl.BlockSpec((D, ec), lambda s: (0, c_item(s) % nchunks)),
                  tab, tab, tab, tab],
        out_specs=pl.BlockSpec((D, tb), lambda s: (0, c_item(s) // nchunks)),
        out_shape=jax.ShapeDtypeStruct((D, T), F32),
        scratch_shapes=[pltpu.VMEM(packed, BF16), pltpu.VMEM(packed, BF16), pltpu.VMEM((2, ec, tb), BF16),
                        pltpu.VMEM((2, ec, tb), F32)],
        compiler_params=_cparams(("arbitrary",)),
        name="peer_dense",
    )(xnt, u, vt, r2, lc, e1, e2)


def _final_kernel(pt_ref, h1_ref, g_ref, out_ref, *, normalize):
    h2 = h1_ref[...] + pt_ref[...].T
    out_ref[...] = _rms(h2, g_ref[...]) if normalize else h2


def _final(pt, h1, g, *, tb, normalize):
    T, D = h1.shape
    return pl.pallas_call(
        functools.partial(_final_kernel, normalize=normalize),
        grid=(T // tb,),
        in_specs=[pl.BlockSpec((D, tb), lambda i: (0, i)), pl.BlockSpec((tb, D), lambda i: (i, 0)),
                  pl.BlockSpec(g.shape, lambda i: (0, 0))],
        out_specs=pl.BlockSpec((tb, D), lambda i: (i, 0)),
        out_shape=jax.ShapeDtypeStruct((T, D), F32),
        compiler_params=_cparams(("parallel",)),
        name="final_norm",
    )(pt, h1, g)


def _rot_cols(w):
    half = w.shape[-1] // 2
    return jnp.concatenate([-w[..., half:], w[..., :half]], axis=-1)


def _layer(h2d, pos2, invf, p, *, B, S):
    T, D = h2d.shape
    w_in = p["w_in"]
    o0, o1, o2, o3 = D, 2 * D, 2 * D + Q_LORA, 2 * D + Q_LORA + KV_LORA + QK_ROPE
    w_kr = w_in[:, o2 + KV_LORA:o3]
    zpad = jnp.zeros((D, LANE - QK_ROPE), F32)
    wkv = jnp.concatenate([w_in[:, o2:o2 + KV_LORA], w_kr, zpad, _rot_cols(w_kr), zpad], axis=1)
    hglu, cq, ckv, gates = _in_proj(
        h2d, p["mix_norm_g"][None, :], w_in[:, :o0].astype(BF16), w_in[:, o0:o1].astype(BF16),
        w_in[:, o1:o2].astype(BF16), wkv.astype(BF16), w_in[:, o3:].astype(BF16), p["b_gate"][None, :],
        tm=TILES.rows)

    w_uq = p["w_uq"].reshape(Q_LORA, N_HEADS, QK_NOPE + QK_ROPE)
    zq = jnp.zeros((Q_LORA, N_HEADS, LANE - QK_ROPE), F32)
    wqa = jnp.concatenate([w_uq, zq], axis=-1).reshape(Q_LORA, N_HEADS * QK_PAD)
    wqr = jnp.concatenate([_rot_cols(w_uq[..., QK_NOPE:]), zq], axis=-1).reshape(Q_LORA, N_HEADS * LANE)
    w_ukv = p["w_ukv"].reshape(KV_LORA, N_HEADS, QK_NOPE + V_DIM)
    wk = w_ukv[..., :QK_NOPE].reshape(KV_LORA, N_HEADS * QK_NOPE)
    wv = w_ukv[..., QK_NOPE:].reshape(KV_LORA, N_HEADS * V_DIM)
    q, k, v = _mla_proj(cq, ckv, pos2, invf, p["q_norm_g"][None, :], wqa.astype(BF16), wqr.astype(BF16),
                        p["kv_norm_g"][None, :], wk.astype(BF16), wv.T.astype(BF16), B=B, S=S, tm=TILES.rows)
    o = _flash_attn(q, k, v, bq=TILES.attn_block, heads=TILES.attn_heads)

    h1 = _mix(hglu, o.reshape(T, D), gates, h2d, p["w_dw"].reshape(CONV_WIDTH, D), p["b_dw"][None, :],
              p["conv_ln_g"][None, :], p["conv_ln_b"][None, :], p["w_conv_out"].astype(BF16),
              p["w_attn_out"].astype(BF16), p["w_o"].astype(BF16), S=S, tm=TILES.conv_rows)

    sk = p["peer_sub_keys"].reshape(2 * PEER_HEADS, N_KEYS, -1).astype(BF16)
    xnt, st = _peer_prep(h1, p["ffn_norm_g"][None, :], p["w_peer_q"].T.astype(BF16), sk, tm=TILES.rows)
    r2, lc, e1, e2 = _peer_topk(st, T=T, tokens=SUB * LANE)
    pt = _peer_dense(xnt, p["peer_u"].astype(BF16), p["peer_v"].T.astype(BF16), r2, lc, e1, e2,
                     tb=TILES.rows, ec=TILES.experts)
    return h1, pt


def kernel(x, positions, mix_norm_g, w_in, b_gate, w_dw, b_dw, conv_ln_g, conv_ln_b, w_conv_out, q_norm_g, w_uq,
           kv_norm_g, w_ukv, w_attn_out, w_o, ffn_norm_g, w_peer_q, peer_sub_keys, peer_u, peer_v, final_norm_g):
    B, S, D = x.shape
    T = B * S
    stacked = dict(mix_norm_g=mix_norm_g, w_in=w_in, b_gate=b_gate, w_dw=w_dw, b_dw=b_dw, conv_ln_g=conv_ln_g,
                   conv_ln_b=conv_ln_b, w_conv_out=w_conv_out, q_norm_g=q_norm_g, w_uq=w_uq, kv_norm_g=kv_norm_g,
                   w_ukv=w_ukv, w_attn_out=w_attn_out, w_o=w_o, ffn_norm_g=ffn_norm_g, w_peer_q=w_peer_q,
                   peer_sub_keys=peer_sub_keys, peer_u=peer_u, peer_v=peer_v)
    depth = w_in.shape[0]
    half = QK_ROPE // 2
    inv_freq = ROPE_BASE ** (-jnp.arange(half, dtype=F32) / half)
    invf = jnp.concatenate([inv_freq, inv_freq, jnp.zeros((LANE - QK_ROPE,), F32)])[None, :]
    pos2 = positions.reshape(T, 1)
    h = x.reshape(T, D)
    for l in range(depth):
        p = {name: a[l] for name, a in stacked.items()}
        h1, pt = _layer(h, pos2, invf, p, B=B, S=S)
        last = l + 1 == depth
        h = _final(pt, h1, final_norm_g[None, :], tb=TILES.rows, normalize=last)
    return h.reshape(B, S, D)
```

```python
import functools
import math
from typing import NamedTuple

import jax
import jax.numpy as jnp
from jax import lax
from jax.experimental import pallas as pl
from jax.experimental.pallas import tpu as pltpu

F32 = jnp.float32
BF16 = jnp.bfloat16

EPS = 1e-6
CONV_WIDTH = 31
N_HEADS = 8
QK_NOPE = 128
QK_ROPE = 64
V_DIM = 128
Q_LORA = 256
KV_LORA = 128
ROPE_BASE = 10000.0
N_KEYS = 128
PEER_HEADS = 8
PEER_TOPK = 16
QK_PAD = 256
LANE = 128
SUB = 8
BF16_ROWS = 16
HALO = 32
VMEM_LIMIT = 56 * 1024 * 1024


class _Tiles(NamedTuple):
    rows: int
    conv_rows: int
    attn_block: int
    attn_heads: int
    experts: int


TILES = _Tiles(rows=512, conv_rows=256, attn_block=512, attn_heads=4, experts=2048)


def _cparams(sem):
    return pltpu.CompilerParams(dimension_semantics=sem, vmem_limit_bytes=VMEM_LIMIT)


def _dot(a, b):
    return jnp.dot(a, b, preferred_element_type=F32)


def _dot_nt(a, b):
    return lax.dot_general(a, b, (((1,), (1,)), ((), ())), preferred_element_type=F32)


def _sigmoid(x):
    return 1.0 / (1.0 + jnp.exp(-x))


def _rms(x, g):
    return x * lax.rsqrt(jnp.mean(x * x, axis=-1, keepdims=True) + EPS) * g


def _inproj_kernel(x_ref, g_ref, wv_ref, wg_ref, wq_ref, wkv_ref, wgl_ref, bg_ref,
                   hglu_ref, cq_ref, ckv_ref, gates_ref):
    a = _rms(x_ref[...], g_ref[...]).astype(BF16)
    val = _dot(a, wv_ref[...])
    gate = _dot(a, wg_ref[...])
    hglu_ref[...] = val * _sigmoid(gate)
    cq_ref[...] = _dot(a, wq_ref[...])
    ckv_ref[...] = _dot(a, wkv_ref[...])
    gates_ref[...] = _sigmoid(_dot(a, wgl_ref[...]) + bg_ref[...]).astype(BF16)


def _in_proj(x2, g, wv, wg, wq, wkv, wgl, bg, *, tm):
    T, D = x2.shape
    full = lambda a: pl.BlockSpec(a.shape, lambda i: (0, 0))
    row = lambda n: pl.BlockSpec((tm, n), lambda i: (i, 0))
    return pl.pallas_call(
        _inproj_kernel,
        grid=(T // tm,),
        in_specs=[row(D), full(g), full(wv), full(wg), full(wq), full(wkv), full(wgl), full(bg)],
        out_specs=[row(wv.shape[1]), row(wq.shape[1]), row(wkv.shape[1]), row(wgl.shape[1])],
        out_shape=[jax.ShapeDtypeStruct((T, wv.shape[1]), F32),
                   jax.ShapeDtypeStruct((T, wq.shape[1]), F32),
                   jax.ShapeDtypeStruct((T, wkv.shape[1]), F32),
                   jax.ShapeDtypeStruct((T, wgl.shape[1]), BF16)],
        compiler_params=_cparams(("parallel",)),
        name="in_proj",
    )(x2, g, wv, wg, wq, wkv, wgl, bg)


def _mla_proj_kernel(cq_ref, ckv_ref, pos_ref, invf_ref, qg_ref, wqa_ref, wqr_ref, kvg_ref, wk_ref, wv_ref,
                     q_ref, k_ref, v_ref, *, scale):
    qn = _rms(cq_ref[...], qg_ref[...]).astype(BF16)
    ckv_full = ckv_ref[...]
    kvn = _rms(ckv_full[:, :KV_LORA], kvg_ref[...]).astype(BF16)
    ang = pos_ref[...].astype(F32) * invf_ref[...]
    cosv = jnp.cos(ang)
    sinv = jnp.sin(ang)
    kr = (ckv_full[:, KV_LORA:KV_LORA + LANE] * cosv + ckv_full[:, KV_LORA + LANE:] * sinv).astype(BF16)
    qa = _dot(qn, wqa_ref[...])
    qr = _dot(qn, wqr_ref[...])
    kn = _dot(kvn, wk_ref[...])
    vt = _dot_nt(wv_ref[...], kvn)
    for h in range(N_HEADS):
        q_ref[0, h, :, 0:LANE] = (qa[:, h * QK_PAD:h * QK_PAD + LANE] * scale).astype(BF16)
        q_rope = qa[:, h * QK_PAD + LANE:(h + 1) * QK_PAD] * cosv + qr[:, h * LANE:(h + 1) * LANE] * sinv
        q_ref[0, h, :, LANE:QK_PAD] = (q_rope * scale).astype(BF16)
        k_ref[0, h, :, 0:LANE] = kn[:, h * LANE:(h + 1) * LANE].astype(BF16)
        k_ref[0, h, :, LANE:QK_PAD] = kr
        v_ref[0, h, :, :] = vt[h * V_DIM:(h + 1) * V_DIM, :].astype(BF16)


def _mla_proj(cq, ckv, pos2, invf, qg, wqa, wqr, kvg, wk, wv, *, B, S, tm):
    T = cq.shape[0]
    nb = S // tm
    full = lambda a: pl.BlockSpec(a.shape, lambda i: (0, 0))
    row = lambda n: pl.BlockSpec((tm, n), lambda i: (i, 0))
    head = lambda n: pl.BlockSpec((1, N_HEADS, tm, n), lambda i: (i // nb, 0, i % nb, 0))
    head_t = pl.BlockSpec((1, N_HEADS, V_DIM, tm), lambda i: (i // nb, 0, 0, i % nb))
    scale = math.log2(math.e) / math.sqrt(QK_NOPE + QK_ROPE)
    return pl.pallas_call(
        functools.partial(_mla_proj_kernel, scale=scale),
        grid=(T // tm,),
        in_specs=[row(cq.shape[1]), row(ckv.shape[1]), row(1), full(invf), full(qg), full(wqa), full(wqr),
                  full(kvg), full(wk), full(wv)],
        out_specs=[head(QK_PAD), head(QK_PAD), head_t],
        out_shape=[jax.ShapeDtypeStruct((B, N_HEADS, S, QK_PAD), BF16),
                   jax.ShapeDtypeStruct((B, N_HEADS, S, QK_PAD), BF16),
                   jax.ShapeDtypeStruct((B, N_HEADS, V_DIM, S), BF16)],
        compiler_params=_cparams(("parallel",)),
        name="mla_proj",
    )(cq, ckv, pos2, invf, qg, wqa, wqr, kvg, wk, wv)


def _flash_kernel(q_ref, k_ref, vt_ref, o_ref, acc_scr, *, bq, bk, heads):
    qi = pl.program_id(2)
    acc_scr[...] = jnp.zeros(acc_scr.shape, F32)

    def step(ki, carry, diagonal):
        off = pl.multiple_of(ki * bk, bk)
        out = []
        for hh in range(heads):
            m_prev, l_prev = carry[2 * hh], carry[2 * hh + 1]
            st = _dot_nt(k_ref[0, hh, pl.ds(off, bk), :], q_ref[0, hh])
            if diagonal:
                kpos = lax.broadcasted_iota(jnp.int32, (bk, bq), 0)
                qpos = lax.broadcasted_iota(jnp.int32, (bk, bq), 1)
                st = jnp.where(kpos <= qpos, st, -1e30)
            m_new = jnp.maximum(m_prev, jnp.max(st, axis=0, keepdims=True))
            p = jnp.exp2(st - m_new)
            alpha = jnp.exp2(m_prev - m_new)
            l_new = alpha * l_prev + jnp.sum(p, axis=0, keepdims=True)
            pv = _dot(vt_ref[0, hh, :, pl.ds(off, bk)], p.astype(BF16))
            acc_scr[hh] = alpha * acc_scr[hh] + pv
            out += [m_new, l_new]
        return tuple(out)

    init = (jnp.full((1, bq), -1e30, F32), jnp.zeros((1, bq), F32)) * heads
    pairs = qi // 2
    carry = lax.fori_loop(0, pairs, lambda kp, c: step(2 * kp + 1, step(2 * kp, c, False), False), init)
    carry = lax.fori_loop(2 * pairs, qi, lambda ki, c: step(ki, c, False), carry)
    carry = step(qi, carry, True)
    for hh in range(heads):
        o_ref[0, :, hh * V_DIM:(hh + 1) * V_DIM] = (acc_scr[hh] / carry[2 * hh + 1]).T.astype(BF16)


def _flash_attn(q, k, vt, *, bq, heads):
    B, H, S, _ = q.shape
    return pl.pallas_call(
        functools.partial(_flash_kernel, bq=bq, bk=bq, heads=heads),
        grid=(B, H // heads, S // bq),
        in_specs=[pl.BlockSpec((1, heads, bq, QK_PAD), lambda b, h, i: (b, h, i, 0)),
                  pl.BlockSpec((1, heads, S, QK_PAD), lambda b, h, i: (b, h, 0, 0)),
                  pl.BlockSpec((1, heads, V_DIM, S), lambda b, h, i: (b, h, 0, 0))],
        out_specs=pl.BlockSpec((1, bq, heads * V_DIM), lambda b, h, i: (b, i, h)),
        out_shape=jax.ShapeDtypeStruct((B, S, H * V_DIM), BF16),
        scratch_shapes=[pltpu.VMEM((heads, V_DIM, bq), F32)],
        compiler_params=_cparams(("parallel", "parallel", "arbitrary")),
        name="flash_attn",
    )(q, k, vt)


def _mix_kernel(hc_ref, hp_ref, o_ref, gates_ref, x_ref, wdw_ref, bdw_ref, lng_ref, lnb_ref,
                wco_ref, wao_ref, wo_ref, h1_ref, hbuf, cbuf, *, tm, blocks_per_seq, rows):
    i = pl.program_id(0)
    first = (i % blocks_per_seq) == 0
    hbuf[0, 0:HALO, :] = jnp.where(first, 0.0, hp_ref[...])
    hbuf[0, HALO:HALO + tm, :] = hc_ref[...]
    D = hc_ref.shape[1]
    span = HALO + tm - SUB
    for s in range(1, SUB):
        hbuf[s, 0:span, :] = hbuf[0, s:s + span, :]
    base = HALO - (CONV_WIDTH - 1)
    for c in range(D // LANE):
        cs = slice(c * LANE, (c + 1) * LANE)
        for r in range(tm // rows):
            acc = jnp.broadcast_to(bdw_ref[:, cs], (rows, LANE))
            for kk in range(CONV_WIDTH):
                shift = (base + kk) % SUB
                start = r * rows + base + kk - shift
                acc = acc + wdw_ref[kk:kk + 1, cs] * hbuf[shift, start:start + rows, cs]
            cbuf[r * rows:(r + 1) * rows, cs] = acc
    y = cbuf[...]
    mu = jnp.mean(y, axis=-1, keepdims=True)
    yc = y - mu
    var = jnp.mean(yc * yc, axis=-1, keepdims=True)
    yn = yc * lax.rsqrt(var + EPS) * lng_ref[...] + lnb_ref[...]
    act = (yn * _sigmoid(yn)).astype(BF16)
    y_conv = _dot(act, wco_ref[...])
    y_attn = _dot(o_ref[...], wao_ref[...])
    gts = gates_ref[...].astype(F32)
    mixed = (gts[:, :D] * y_conv + gts[:, D:] * y_attn).astype(BF16)
    h1_ref[...] = x_ref[...] + _dot(mixed, wo_ref[...])


def _mix(hglu, o2, gates, x2, wdw, bdw, lng, lnb, wco, wao, wo, *, S, tm):
    T, D = x2.shape
    full = lambda a: pl.BlockSpec(a.shape, lambda i: (0, 0))
    row = lambda n: pl.BlockSpec((tm, n), lambda i: (i, 0))
    per = tm // HALO
    halo = pl.BlockSpec((HALO, D), lambda i: (jnp.maximum(i * per - 1, 0), 0))
    return pl.pallas_call(
        functools.partial(_mix_kernel, tm=tm, blocks_per_seq=S // tm, rows=64),
        grid=(T // tm,),
        in_specs=[row(D), halo, row(D), row(2 * D), row(D), full(wdw), full(bdw), full(lng), full(lnb),
                  full(wco), full(wao), full(wo)],
        out_specs=row(D),
        out_shape=jax.ShapeDtypeStruct((T, D), F32),
        scratch_shapes=[pltpu.VMEM((SUB, HALO + tm, D), F32), pltpu.VMEM((tm, D), F32)],
        compiler_params=_cparams(("parallel",)),
        name="mix",
    )(hglu, hglu, o2, gates, x2, wdw, bdw, lng, lnb, wco, wao, wo)


def _peer_prep_kernel(h1_ref, g_ref, wpqt_ref, sk_ref, xnt_ref, st_ref):
    tm = h1_ref.shape[0]
    xn32 = _rms(h1_ref[...], g_ref[...])
    xnt_ref[...] = xn32.T.astype(BF16)
    xn = xn32.astype(BF16)
    qt = _dot_nt(wpqt_ref[...], xn).astype(BF16)
    groups = N_KEYS // SUB
    for hp in range(2 * PEER_HEADS):
        rs = slice(hp * N_KEYS, (hp + 1) * N_KEYS)
        s = _dot(sk_ref[hp], qt[rs, :])
        for a in range(tm // LANE):
            st_ref[hp * groups:(hp + 1) * groups, a * SUB:(a + 1) * SUB, :] = (
                s[:, a * LANE:(a + 1) * LANE].reshape(groups, SUB, LANE))


def _peer_prep(h1, g, wpqt, sk, *, tm):
    T, D = h1.shape
    G = wpqt.shape[0] // SUB
    return pl.pallas_call(
        _peer_prep_kernel,
        grid=(T // tm,),
        in_specs=[pl.BlockSpec((tm, D), lambda i: (i, 0)), pl.BlockSpec(g.shape, lambda i: (0, 0)),
                  pl.BlockSpec(wpqt.shape, lambda i: (0, 0)), pl.BlockSpec(sk.shape, lambda i: (0, 0, 0))],
        out_specs=[pl.BlockSpec((D, tm), lambda i: (0, i)),
                   pl.BlockSpec((G, tm // LANE * SUB, LANE), lambda i: (0, i, 0))],
        out_shape=[jax.ShapeDtypeStruct((D, T), BF16), jax.ShapeDtypeStruct((G, T // LANE * SUB, LANE), F32)],
        compiler_params=_cparams(("parallel",)),
        name="peer_prep",
    )(h1, g, wpqt, sk)


def _tree(op, xs):
    xs = list(xs)
    while len(xs) > 1:
        xs = [op(xs[i], xs[i + 1]) for i in range(0, len(xs) - 1, 2)] + ([xs[-1]] if len(xs) % 2 else [])
    return xs[0]


def _first_index(xs, m, chains):
    n = len(xs)
    per = n // chains
    heads = []
    for c in range(chains):
        idx = jnp.full(m.shape, float(n), F32)
        for i in reversed(range(c * per, (c + 1) * per)):
            idx = jnp.where(xs[i] == m, float(i), idx)
        heads.append(idx)
    return _tree(jnp.minimum, heads)


def _peer_topk_kernel(st_ref, r2_ref, lc_ref, e1_ref, e2_ref, work, rank2, vals, idx1):
    K = PEER_TOPK

    def rows(k):
        return k // SUB, pl.ds(k % SUB, SUB, stride=SUB)

    def score(p, k):
        g, r = rows(p * N_KEYS + k)
        return st_ref[g, r, :]

    for p in range(2):
        for k in range(N_KEYS):
            work[p, k] = score(p, k)
    for k in range(N_KEYS):
        rank2[k] = jnp.full((SUB, LANE), float(K), F32)

    def extract(a, carry):
        af = lax.convert_element_type(a, F32)
        for p in range(2):
            w = [work[p, k] for k in range(N_KEYS)]
            m = _tree(jnp.maximum, w)
            idx = _first_index(w, m, 8)
            vals[p, a] = m
            if p == 0:
                idx1[a] = idx
            for k in range(N_KEYS):
                sel = idx == float(k)
                work[p, k] = jnp.where(sel, -jnp.inf, w[k])
                if p == 1:
                    rank2[k] = jnp.where(sel, af, rank2[k])
        return carry

    lax.fori_loop(0, K, extract, 0)

    v1 = [vals[0, a] for a in range(K)]
    v2 = [vals[1, b] for b in range(K)]
    plen = [jnp.zeros((SUB, LANE), F32) for _ in range(K)]
    front = [v1[a] + v2[0] for a in range(K)]
    for _ in range(K):
        m = _tree(jnp.maximum, front)
        a_star = _first_index(front, m, 2)
        lsel = jnp.zeros((SUB, LANE), F32)
        v1sel = jnp.zeros((SUB, LANE), F32)
        sels = []
        for a in range(K):
            sel = a_star == float(a)
            sels.append(sel)
            plen[a] = plen[a] + jnp.where(sel, 1.0, 0.0)
            lsel = jnp.where(sel, plen[a], lsel)
            v1sel = jnp.where(sel, v1[a], v1sel)
        v2next = jnp.zeros((SUB, LANE), F32)
        for b in range(1, K):
            v2next = jnp.where(lsel == float(b), v2[b], v2next)
        fnew = jnp.where(lsel >= float(K), -jnp.inf, v1sel + v2next)
        for a in range(K):
            front[a] = jnp.where(sels[a], fnew, front[a])
    e1s = [jnp.exp(v1[a] - v1[0]) for a in range(K)]
    z = jnp.zeros((SUB, LANE), F32)
    for b in range(K):
        inner = jnp.zeros((SUB, LANE), F32)
        for a in range(K):
            inner = inner + jnp.where(plen[a] > float(b), e1s[a], 0.0)
        z = z + jnp.exp(v2[b] - v2[0]) * inner
    rz = 1.0 / z
    ids = [idx1[a] for a in range(K)]
    for k in range(N_KEYS):
        g, r = rows(k)
        lc = jnp.zeros((SUB, LANE), F32)
        for a in range(K):
            lc = jnp.where(ids[a] == float(k), plen[a], lc)
        lc_ref[0, g, r, :] = lc
        e1_ref[0, g, r, :] = jnp.exp(score(0, k) - v1[0]) * rz
        r2_ref[0, g, r, :] = rank2[k]
        e2_ref[0, g, r, :] = jnp.exp(score(1, k) - v2[0])


def _peer_topk(st, *, T, tokens):
    groups = N_KEYS // SUB
    rows_per_step = tokens // LANE * SUB
    tab = pl.BlockSpec((1, groups, rows_per_step, LANE), lambda t, h: (h, 0, t, 0))
    shp = jax.ShapeDtypeStruct((PEER_HEADS, groups, T // LANE * SUB, LANE), F32)
    vreg = (SUB, LANE)
    return pl.pallas_call(
        _peer_topk_kernel,
        grid=(T // tokens, PEER_HEADS),
        in_specs=[pl.BlockSpec((2 * groups, rows_per_step, LANE), lambda t, h: (h, t, 0))],
        out_specs=[tab, tab, tab, tab],
        out_shape=[shp, shp, shp, shp],
        scratch_shapes=[pltpu.VMEM((2, N_KEYS) + vreg, F32), pltpu.VMEM((N_KEYS,) + vreg, F32),
                        pltpu.VMEM((2, PEER_TOPK) + vreg, F32), pltpu.VMEM((PEER_TOPK,) + vreg, F32)],
        compiler_params=_cparams(("parallel", "parallel")),
        name="peer_topk",
    )(st)


def _gelu(a):
    return 0.5 * a * (1.0 + lax.erf(a * (1.0 / math.sqrt(2.0))))


def _interleave(major, minor):
    out, done = [], 0
    for n, item in enumerate(major):
        out.append(item)
        want = (n + 1) * len(minor) // len(major)
        out += minor[done:want]
        done = want
    return out


def _peer_dense_kernel(xnt_ref, u_ref, vt_ref, r2_ref, lc_ref, e1_ref, e2_ref, out_ref, r2b, e2b, w_scr, act_scr,
                       *, nchunks, nwork):
    s = pl.program_id(0)
    D, tb = xnt_ref.shape
    ec = u_ref.shape[0]
    tiles = ec // N_KEYS
    nl = tb // LANE
    packed = (N_KEYS // BF16_ROWS, BF16_ROWS, LANE)
    w_item = jnp.clip(s - 1, 0, nwork - 1)
    c_item = jnp.clip(s - 2, 0, nwork - 1)
    chunk = w_item % nchunks

    @pl.when(s == 0)
    def _():
        w_scr[...] = jnp.zeros(w_scr.shape, BF16)
        act_scr[...] = jnp.zeros(act_scr.shape, F32)

    @pl.when(chunk == 0)
    def _():
        for h in range(PEER_HEADS):
            for c in range(nl):
                r2b[h, c] = r2_ref[h, :, c * SUB:(c + 1) * SUB, :].reshape(packed).astype(BF16)
                e2b[h, c] = e2_ref[h, :, c * SUB:(c + 1) * SUB, :].reshape(packed).astype(BF16)

    @pl.when(c_item % nchunks == 0)
    def _():
        out_ref[...] = jnp.zeros(out_ref.shape, F32)

    cur = s % 2
    prev = 1 - cur
    zero = jnp.zeros((), BF16)
    halves = 2
    hw = tb // halves
    quarters = 4
    dp = D // quarters
    share = 2

    def scores(n):
        cs = slice(n * hw, (n + 1) * hw)
        act_scr[cur, :, cs] = _dot(u_ref[...], xnt_ref[:, cs])

    def combine(n):
        rs = slice(n * dp, (n + 1) * dp)
        out_ref[rs, :] += lax.dot_general(vt_ref[:, rs], w_scr[cur], (((0,), (0,)), ((), ())),
                                          preferred_element_type=F32)

    def weights(i0, c):
        gs = [jnp.zeros(packed, BF16) for _ in range(share)]
        for h in range(PEER_HEADS):
            r2 = r2b[h, c]
            e2 = e2b[h, c]
            for n in range(share):
                grp = chunk * (tiles // SUB) + (i0 + n) // SUB
                row = c * SUB + (i0 + n) % SUB
                lc = jnp.broadcast_to(lc_ref[h, pl.ds(grp, 1), row, :], (BF16_ROWS, LANE)).astype(BF16)
                e1 = jnp.broadcast_to(e1_ref[h, pl.ds(grp, 1), row, :], (BF16_ROWS, LANE)).astype(BF16)
                gs[n] = gs[n] + jnp.where(r2 < lc[None], e2, zero) * e1[None]
        cs = slice(c * LANE, (c + 1) * LANE)
        for n in range(share):
            rs = slice((i0 + n) * N_KEYS, (i0 + n + 1) * N_KEYS)
            a = _gelu(act_scr[prev, rs, cs].astype(BF16))
            w_scr[prev, rs, cs] = gs[n].reshape(N_KEYS, LANE) * a

    mxu = [functools.partial(combine, 0), functools.partial(scores, 0), functools.partial(combine, 1),
           functools.partial(combine, 2), functools.partial(scores, 1), functools.partial(combine, 3)]
    vpu = [functools.partial(weights, i0, c) for i0 in range(0, tiles, share) for c in range(nl)]
    for stage in _interleave(mxu, vpu):
        stage()


def _peer_dense(xnt, u, vt, r2, lc, e1, e2, *, tb, ec):
    D, T = xnt.shape
    E = u.shape[0]
    assert ec % (N_KEYS * SUB) == 0
    nchunks = E // ec
    nwork = (T // tb) * nchunks
    groups = N_KEYS // SUB
    nl = tb // LANE
    s_item = lambda s: jnp.minimum(s, nwork - 1)
    w_item = lambda s: jnp.clip(s - 1, 0, nwork - 1)
    c_item = lambda s: jnp.clip(s - 2, 0, nwork - 1)
    tab = pl.BlockSpec((PEER_HEADS, groups, nl * SUB, LANE), lambda s: (0, 0, w_item(s) // nchunks, 0))
    packed = (PEER_HEADS, nl, N_KEYS // BF16_ROWS, BF16_ROWS, LANE)
    return pl.pallas_call(
        functools.partial(_peer_dense_kernel, nchunks=nchunks, nwork=nwork),
        grid=(nwork + 2,),
        in_specs=[pl.BlockSpec((D, tb), lambda s: (0, s_item(s) // nchunks)),
                  pl.BlockSpec((ec, D), lambda s: (s_item(s) % nchunks, 0)),
                  pl.BlockSpec((ec, D), lambda s: (c_item(s) % nchunks, 0)),
                  tab, tab, tab, tab],
        out_specs=pl.BlockSpec((D, tb), lambda s: (0, c_item(s) // nchunks)),
        out_shape=jax.ShapeDtypeStruct((D, T), F32),
        scratch_shapes=[pltpu.VMEM(packed, BF16), pltpu.VMEM(packed, BF16), pltpu.VMEM((2, ec, tb), BF16),
                        pltpu.VMEM((2, ec, tb), F32)],
        compiler_params=_cparams(("arbitrary",)),
        name="peer_dense",
    )(xnt, u, vt, r2, lc, e1, e2)


def _final_kernel(pt_ref, h1_ref, g_ref, out_ref, *, normalize):
    h2 = h1_ref[...] + pt_ref[...].T
    out_ref[...] = _rms(h2, g_ref[...]) if normalize else h2


def _final(pt, h1, g, *, tb, normalize):
    T, D = h1.shape
    return pl.pallas_call(
        functools.partial(_final_kernel, normalize=normalize),
        grid=(T // tb,),
        in_specs=[pl.BlockSpec((D, tb), lambda i: (0, i)), pl.BlockSpec((tb, D), lambda i: (i, 0)),
                  pl.BlockSpec(g.shape, lambda i: (0, 0))],
        out_specs=pl.BlockSpec((tb, D), lambda i: (i, 0)),
        out_shape=jax.ShapeDtypeStruct((T, D), F32),
        compiler_params=_cparams(("parallel",)),
        name="final_norm",
    )(pt, h1, g)


def _rot_cols(w):
    half = w.shape[-1] // 2
    return jnp.concatenate([-w[..., half:], w[..., :half]], axis=-1)


def _layer(h2d, pos2, invf, p, *, B, S):
    T, D = h2d.shape
    w_in = p["w_in"]
    o0, o1, o2, o3 = D, 2 * D, 2 * D + Q_LORA, 2 * D + Q_LORA + KV_LORA + QK_ROPE
    w_kr = w_in[:, o2 + KV_LORA:o3]
    zpad = jnp.zeros((D, LANE - QK_ROPE), F32)
    wkv = jnp.concatenate([w_in[:, o2:o2 + KV_LORA], w_kr, zpad, _rot_cols(w_kr), zpad], axis=1)
    hglu, cq, ckv, gates = _in_proj(
        h2d, p["mix_norm_g"][None, :], w_in[:, :o0].astype(BF16), w_in[:, o0:o1].astype(BF16),
        w_in[:, o1:o2].astype(BF16), wkv.astype(BF16), w_in[:, o3:].astype(BF16), p["b_gate"][None, :],
        tm=TILES.rows)

    w_uq = p["w_uq"].reshape(Q_LORA, N_HEADS, QK_NOPE + QK_ROPE)
    zq = jnp.zeros((Q_LORA, N_HEADS, LANE - QK_ROPE), F32)
    wqa = jnp.concatenate([w_uq, zq], axis=-1).reshape(Q_LORA, N_HEADS * QK_PAD)
    wqr = jnp.concatenate([_rot_cols(w_uq[..., QK_NOPE:]), zq], axis=-1).reshape(Q_LORA, N_HEADS * LANE)
    w_ukv = p["w_ukv"].reshape(KV_LORA, N_HEADS, QK_NOPE + V_DIM)
    wk = w_ukv[..., :QK_NOPE].reshape(KV_LORA, N_HEADS * QK_NOPE)
    wv = w_ukv[..., QK_NOPE:].reshape(KV_LORA, N_HEADS * V_DIM)
    q, k, v = _mla_proj(cq, ckv, pos2, invf, p["q_norm_g"][None, :], wqa.astype(BF16), wqr.astype(BF16),
                        p["kv_norm_g"][None, :], wk.astype(BF16), wv.T.astype(BF16), B=B, S=S, tm=TILES.rows)
    o = _flash_attn(q, k, v, bq=TILES.attn_block, heads=TILES.attn_heads)

    h1 = _mix(hglu, o.reshape(T, D), gates, h2d, p["w_dw"].reshape(CONV_WIDTH, D), p["b_dw"][None, :],
              p["conv_ln_g"][None, :], p["conv_ln_b"][None, :], p["w_conv_out"].astype(BF16),
              p["w_attn_out"].astype(BF16), p["w_o"].astype(BF16), S=S, tm=TILES.conv_rows)

    sk = p["peer_sub_keys"].reshape(2 * PEER_HEADS, N_KEYS, -1).astype(BF16)
    xnt, st = _peer_prep(h1, p["ffn_norm_g"][None, :], p["w_peer_q"].T.astype(BF16), sk, tm=TILES.rows)
    r2, lc, e1, e2 = _peer_topk(st, T=T, tokens=SUB * LANE)
    pt = _peer_dense(xnt, p["peer_u"].astype(BF16), p["peer_v"].astype(BF16), r2, lc, e1, e2,
                     tb=TILES.rows, ec=TILES.experts)
    return h1, pt


def kernel(x, positions, mix_norm_g, w_in, b_gate, w_dw, b_dw, conv_ln_g, conv_ln_b, w_conv_out, q_norm_g, w_uq,
           kv_norm_g, w_ukv, w_attn_out, w_o, ffn_norm_g, w_peer_q, peer_sub_keys, peer_u, peer_v, final_norm_g):
    B, S, D = x.shape
    T = B * S
    stacked = dict(mix_norm_g=mix_norm_g, w_in=w_in, b_gate=b_gate, w_dw=w_dw, b_dw=b_dw, conv_ln_g=conv_ln_g,
                   conv_ln_b=conv_ln_b, w_conv_out=w_conv_out, q_norm_g=q_norm_g, w_uq=w_uq, kv_norm_g=kv_norm_g,
                   w_ukv=w_ukv, w_attn_out=w_attn_out, w_o=w_o, ffn_norm_g=ffn_norm_g, w_peer_q=w_peer_q,
                   peer_sub_keys=peer_sub_keys, peer_u=peer_u, peer_v=peer_v)
    depth = w_in.shape[0]
    half = QK_ROPE // 2
    inv_freq = ROPE_BASE ** (-jnp.arange(half, dtype=F32) / half)
    invf = jnp.concatenate([inv_freq, inv_freq, jnp.zeros((LANE - QK_ROPE,), F32)])[None, :]
    pos2 = positions.reshape(T, 1)
    h = x.reshape(T, D)
    for l in range(depth):
        p = {name: a[l] for name, a in stacked.items()}
        h1, pt = _layer(h, pos2, invf, p, B=B, S=S)
        last = l + 1 == depth
        h = _final(pt, h1, final_norm_g[None, :], tb=TILES.rows, normalize=last)
    return h.reshape(B, S, D)
```
